```python
import math
import jax, jax.numpy as jnp
from jax import lax
import numpy as np

D_MODEL = 2048
BATCH = 8
SEQ = 2048
DEPTH = 1

GRID_W = 64
CTX_LEN = 256
HEAD_DIM = 128
N_HEADS = D_MODEL // HEAD_DIM
N_KV_HEADS = N_HEADS // 4
WINDOW = 128
BLOCK = 128
C_CONV = D_MODEL
CONV_K = 31
FFN_HIDDEN = -(-8 * D_MODEL // (3 * 256)) * 256
ROPE_BASE = 10000.0
ALPHA = (2.0 * DEPTH) ** 0.25
BETA = (8.0 * DEPTH) ** -0.25
EPS = 1e-6
NEG_INF = -1e30

Q_COLS = N_HEADS * HEAD_DIM
KV_COLS = N_KV_HEADS * HEAD_DIM
OFF_K = Q_COLS
OFF_V = OFF_K + KV_COLS
OFF_GLU = OFF_V + KV_COLS
OFF_GATE = OFF_GLU + 2 * C_CONV
IN_COLS = OFF_GATE + 2 * D_MODEL

kernel_name = 'hybrid_gqa_conformer_dit_layer'


def layer_norm(x, g=None, b=None):
    xf = x.astype(jnp.float32)
    mu = jnp.mean(xf, axis=-1, keepdims=True)
    var = jnp.mean(jnp.square(xf - mu), axis=-1, keepdims=True)
    y = (xf - mu) * lax.rsqrt(var + EPS)
    if g is not None:
        y = y * g.astype(jnp.float32) + b.astype(jnp.float32)
    return y.astype(x.dtype)


def modulate(x, shift, scale):
    return x * (1.0 + scale) + shift


def heads(t, n):
    return t.reshape(t.shape[:-1] + (n, HEAD_DIM))


def axial_rope_tables(n_tokens):
    rows = n_tokens // GRID_W
    row = jnp.repeat(jnp.arange(rows, dtype=jnp.int32), GRID_W)
    col = jnp.tile(jnp.arange(GRID_W, dtype=jnp.int32), rows)
    n_freq = HEAD_DIM // 4
    inv_freq = ROPE_BASE ** (-jnp.arange(n_freq, dtype=jnp.float32) / n_freq)
    ang = jnp.stack([row.astype(jnp.float32)[:, None] * inv_freq,
                     col.astype(jnp.float32)[:, None] * inv_freq], axis=1)
    return jnp.cos(ang), jnp.sin(ang)


def apply_axial_rope(x, cos, sin):
    B, S, H, _ = x.shape
    xr = x.astype(jnp.float32).reshape(B, S, H, 2, 2, HEAD_DIM // 4)
    x1, x2 = xr[..., 0, :], xr[..., 1, :]
    c = cos[None, :, None]
    s = sin[None, :, None]
    out = jnp.stack([x1 * c - x2 * s, x1 * s + x2 * c], axis=-2)
    return out.reshape(B, S, H, HEAD_DIM).astype(x.dtype)


def latent_window_attention(q, k, v, k_ctx, v_ctx, sink):
    B, S, H, dh = q.shape
    nb = S // BLOCK
    G = H // N_KV_HEADS
    scale = HEAD_DIM ** -0.5
    qb = jnp.moveaxis(q.reshape(B, nb, BLOCK, N_KV_HEADS, G, dh), 1, 0)
    pad = ((0, 0), (BLOCK, BLOCK), (0, 0), (0, 0))

    def band(t):
        tb = jnp.pad(t, pad).reshape(B, nb + 2, BLOCK, N_KV_HEADS, dh)
        tb = jnp.concatenate([tb[:, :-2], tb[:, 1:-1], tb[:, 2:]], axis=2)
        return jnp.moveaxis(tb, 1, 0)

    kb, vb = band(k), band(v)
    sink_l = sink.astype(jnp.float32).reshape(N_KV_HEADS, G)[None, :, :, None, None]
    q_off = jnp.arange(BLOCK, dtype=jnp.int32)
    k_off = jnp.arange(3 * BLOCK, dtype=jnp.int32)

    def per_block(args):
        qblk, kblk, vblk, bidx = args
        i_abs = bidx * BLOCK + q_off
        j_abs = bidx * BLOCK - BLOCK + k_off
        valid = (jnp.abs(j_abs[None, :] - i_abs[:, None]) <= WINDOW) & (j_abs >= 0)[None, :] & (j_abs < S)[None, :]
        s_win = jnp.einsum('bqkgd,bskd->bkgqs', qblk, kblk, preferred_element_type=jnp.float32) * scale
        s_win = jnp.where(valid, s_win, NEG_INF)
        s_ctx = jnp.einsum('bqkgd,blkd->bkgql', qblk, k_ctx, preferred_element_type=jnp.float32) * scale
        s_sink = jnp.broadcast_to(sink_l, s_win.shape[:-1] + (1,))
        p = jax.nn.softmax(jnp.concatenate([s_win, s_ctx, s_sink], axis=-1), axis=-1)
        p_win = p[..., :3 * BLOCK].astype(vblk.dtype)
        p_ctx = p[..., 3 * BLOCK:3 * BLOCK + k_ctx.shape[1]].astype(v_ctx.dtype)
        return (jnp.einsum('bkgqs,bskd->bqkgd', p_win, vblk)
                + jnp.einsum('bkgql,blkd->bqkgd', p_ctx, v_ctx))

    out = lax.map(per_block, (qb, kb, vb, jnp.arange(nb, dtype=jnp.int32)))
    return jnp.moveaxis(out, 0, 1).reshape(B, S, H * dh)


def context_attention(q, k, v, sink):
    B, L, H, dh = q.shape
    G = H // N_KV_HEADS
    qg = q.reshape(B, L, N_KV_HEADS, G, dh)
    s = jnp.einsum('bqkgd,bskd->bkgqs', qg, k, preferred_element_type=jnp.float32) * (HEAD_DIM ** -0.5)
    s_sink = jnp.broadcast_to(sink.astype(jnp.float32).reshape(N_KV_HEADS, G)[None, :, :, None, None], s.shape[:-1] + (1,))
    p = jax.nn.softmax(jnp.concatenate([s, s_sink], axis=-1), axis=-1)[..., :-1]
    return jnp.einsum('bkgqs,bskd->bqkgd', p.astype(v.dtype), v).reshape(B, L, H * dh)


def conformer_conv(glu_in, conv_w, conv_b, norm_g, norm_b, w_proj):
    a, gt = jnp.split(glu_in, 2, axis=-1)
    u = a * jax.nn.sigmoid(gt)
    y = lax.conv_general_dilated(u, conv_w[:, None, :], window_strides=(1,),
                                 padding=[(CONV_K // 2, CONV_K // 2)],
                                 dimension_numbers=('NWC', 'WIO', 'NWC'),
                                 feature_group_count=C_CONV) + conv_b
    y = jax.nn.silu(layer_norm(y, norm_g, norm_b))
    return y @ w_proj


def merge_branches(attn_flat, glu_in, gate_logits, lp):
    y_attn = attn_flat @ lp['w_attn_proj']
    y_conv = conformer_conv(glu_in, lp['conv_w'], lp['conv_b'], lp['conv_norm_g'], lp['conv_norm_b'], lp['w_conv_proj'])
    g_attn, g_conv = jnp.split(jax.nn.sigmoid(gate_logits), 2, axis=-1)
    return (g_attn * y_attn + g_conv * y_conv) @ lp['w_out']


def swiglu(h, w_in, w_out):
    g, u = jnp.split(h @ w_in, 2, axis=-1)
    return (jax.nn.silu(g) * u) @ w_out


def hybrid_layer(x, xc, mod, mod_c, lp, cos, sin, update_ctx):
    sh1, sc1, g1, sh2, sc2, g2 = jnp.split(mod, 6, axis=-1)
    csh1, csc1, cg1, csh2, csc2, cg2 = jnp.split(mod_c, 6, axis=-1)
    w_in = lp['w_in']
    h = modulate(x, sh1, sc1)
    hc = modulate(xc, csh1, csc1)
    kv_c = hc @ w_in[:, OFF_K:OFF_GLU]
    k_c = heads(kv_c[..., :KV_COLS], N_KV_HEADS)
    v_c = heads(kv_c[..., KV_COLS:], N_KV_HEADS)
    proj = h @ w_in
    q = apply_axial_rope(heads(proj[..., :OFF_K], N_HEADS), cos, sin)
    k = apply_axial_rope(heads(proj[..., OFF_K:OFF_V], N_KV_HEADS), cos, sin)
    v = heads(proj[..., OFF_V:OFF_GLU], N_KV_HEADS)
    attn = latent_window_attention(q, k, v, k_c, v_c, lp['attn_sink'])
    mix = merge_branches(attn, proj[..., OFF_GLU:OFF_GATE], proj[..., OFF_GATE:], lp)
    x_new = layer_norm(ALPHA * x + g1 * mix, lp['ln1_g'], lp['ln1_b'])
    ffn = swiglu(modulate(x_new, sh2, sc2), lp['w_ffn_in'], lp['w_ffn_out'])
    x_new = layer_norm(ALPHA * x_new + g2 * ffn, lp['ln2_g'], lp['ln2_b'])
    if update_ctx:
        q_c = heads(hc @ w_in[:, :OFF_K], N_HEADS)
        rest_c = hc @ w_in[:, OFF_GLU:]
        attn_c = context_attention(q_c, k_c, v_c, lp['attn_sink'])
        mix_c = merge_branches(attn_c, rest_c[..., :2 * C_CONV], rest_c[..., 2 * C_CONV:], lp)
        xc = layer_norm(ALPHA * xc + cg1 * mix_c, lp['ln1_g'], lp['ln1_b'])
        ffn_c = swiglu(modulate(xc, csh2, csc2), lp['w_ffn_in'], lp['w_ffn_out'])
        xc = layer_norm(ALPHA * xc + cg2 * ffn_c, lp['ln2_g'], lp['ln2_b'])
    return x_new, xc


def setup_inputs(seed: int = 0) -> dict:
    key = jax.random.key(seed)
    ks = jax.random.split(key, 24)
    f32 = jnp.float32

    def nrm(k, shape, scale):
        return jax.random.normal(k, shape, f32) * scale

    col_scale = jnp.ones((IN_COLS,), f32).at[OFF_V:OFF_GLU].set(BETA)
    return {
        'x': nrm(ks[0], (BATCH, SEQ, D_MODEL), 1.0),
        'c': nrm(ks[1], (BATCH, D_MODEL), 1.0),
        'ctx': nrm(ks[2], (BATCH, CTX_LEN, D_MODEL), 1.0),
        'c_ctx': nrm(ks[3], (D_MODEL,), 1.0),
        'w_mod': nrm(ks[4], (DEPTH, D_MODEL, 6 * D_MODEL), 0.5 * D_MODEL ** -0.5),
        'b_mod': nrm(ks[5], (DEPTH, 6 * D_MODEL), 0.02),
        'w_in': nrm(ks[6], (DEPTH, D_MODEL, IN_COLS), D_MODEL ** -0.5) * col_scale,
        'attn_sink': nrm(ks[7], (DEPTH, N_HEADS), 0.5),
        'conv_w': nrm(ks[8], (DEPTH, CONV_K, C_CONV), CONV_K ** -0.5),
        'conv_b': nrm(ks[9], (DEPTH, C_CONV), 0.02),
        'conv_norm_g': 1.0 + nrm(ks[10], (DEPTH, C_CONV), 0.02),
        'conv_norm_b': nrm(ks[11], (DEPTH, C_CONV), 0.02),
        'w_attn_proj': nrm(ks[12], (DEPTH, Q_COLS, D_MODEL), BETA * Q_COLS ** -0.5),
        'w_conv_proj': nrm(ks[13], (DEPTH, C_CONV, D_MODEL), BETA * C_CONV ** -0.5),
        'w_out': nrm(ks[14], (DEPTH, D_MODEL, D_MODEL), BETA * D_MODEL ** -0.5),
        'ln1_g': 1.0 + nrm(ks[15], (DEPTH, D_MODEL), 0.02),
        'ln1_b': nrm(ks[16], (DEPTH, D_MODEL), 0.02),
        'w_ffn_in': nrm(ks[17], (DEPTH, D_MODEL, 2 * FFN_HIDDEN), BETA * D_MODEL ** -0.5),
        'w_ffn_out': nrm(ks[18], (DEPTH, FFN_HIDDEN, D_MODEL), BETA * FFN_HIDDEN ** -0.5),
        'ln2_g': 1.0 + nrm(ks[19], (DEPTH, D_MODEL), 0.02),
        'ln2_b': nrm(ks[20], (DEPTH, D_MODEL), 0.02),
    }


def reference(x, c, ctx, c_ctx, w_mod, b_mod, w_in, attn_sink, conv_w, conv_b, conv_norm_g, conv_norm_b,
              w_attn_proj, w_conv_proj, w_out, ln1_g, ln1_b, w_ffn_in, w_ffn_out, ln2_g, ln2_b):
    cos, sin = axial_rope_tables(x.shape[1])
    x = layer_norm(x)
    xc = layer_norm(ctx)
    silu_c = jax.nn.silu(c)
    silu_cc = jax.nn.silu(c_ctx)
    for i in range(DEPTH):
        mod = (silu_c @ w_mod[i] + b_mod[i])[:, None, :]
        mod_c = silu_cc @ w_mod[i] + b_mod[i]
        lp = {
            'w_in': w_in[i], 'attn_sink': attn_sink[i],
            'conv_w': conv_w[i], 'conv_b': conv_b[i],
            'conv_norm_g': conv_norm_g[i], 'conv_norm_b': conv_norm_b[i],
            'w_attn_proj': w_attn_proj[i], 'w_conv_proj': w_conv_proj[i], 'w_out': w_out[i],
            'ln1_g': ln1_g[i], 'ln1_b': ln1_b[i],
            'w_ffn_in': w_ffn_in[i], 'w_ffn_out': w_ffn_out[i],
            'ln2_g': ln2_g[i], 'ln2_b': ln2_b[i],
        }
        x, xc = hybrid_layer(x, xc, mod, mod_c, lp, cos, sin, update_ctx=(i < DEPTH - 1))
    return x
```

```python
import functools

import jax
import jax.numpy as jnp
from jax import lax
from jax.experimental import pallas as pl
from jax.experimental.pallas import tpu as pltpu

D_MODEL = 2048
GRID_W = 64
HEAD_DIM = 128
N_HEADS = 16
N_KV_HEADS = 4
GROUP = N_HEADS // N_KV_HEADS
BLOCK = 128
CONV_K = 31
CONV_HALO = 16
SUB = 8
LANES = 128
ROPE_BASE = 10000.0
DEPTH = 1
ALPHA = (2.0 * DEPTH) ** 0.25
EPS = 1e-6
NEG_INF = -1e30
SCALE = HEAD_DIM ** -0.5

Q_COLS = N_HEADS * HEAD_DIM
KV_COLS = N_KV_HEADS * HEAD_DIM
OFF_K = Q_COLS
OFF_V = OFF_K + KV_COLS
OFF_GLU = OFF_V + KV_COLS
OFF_GATE = OFF_GLU + 2 * D_MODEL

VMEM_LIMIT = 56 * 1024 * 1024

_BF16 = jnp.bfloat16
_F32 = jnp.float32


def _params(sem):
    return pltpu.CompilerParams(dimension_semantics=sem, vmem_limit_bytes=VMEM_LIMIT)


def _ln(xf):
    mu = jnp.mean(xf, axis=-1, keepdims=True)
    xc = xf - mu
    var = jnp.mean(xc * xc, axis=-1, keepdims=True)
    return xc * lax.rsqrt(var + EPS)


def _mod_kernel(c_ref, w_ref, b_ref, o_ref):
    c = c_ref[...]
    s = (c * jax.nn.sigmoid(c)).astype(_BF16)
    o_ref[...] = jnp.dot(s, w_ref[...].astype(_BF16), preferred_element_type=_F32) + b_ref[...]


def _mod_call(c_all, w_mod, b_mod):
    rows, d = c_all.shape
    n = w_mod.shape[1]
    tn = 1024
    return pl.pallas_call(
        _mod_kernel,
        out_shape=jax.ShapeDtypeStruct((rows, n), _F32),
        grid=(n // tn,),
        in_specs=[pl.BlockSpec((rows, d), lambda j: (0, 0)),
                  pl.BlockSpec((d, tn), lambda j: (0, j)),
                  pl.BlockSpec((1, tn), lambda j: (0, j))],
        out_specs=pl.BlockSpec((rows, tn), lambda j: (0, j)),
        compiler_params=_params(("arbitrary",)),
        name="mod",
    )(c_all, w_mod, b_mod)


def _ln_mod_kernel(x_ref, mod_ref, o_ref, *, shift_row, scale_row):
    y = _ln(x_ref[...])
    shift = mod_ref[0, shift_row:shift_row + 1, :]
    scale = mod_ref[0, scale_row:scale_row + 1, :]
    o_ref[...] = (y * (1.0 + scale) + shift).astype(o_ref.dtype)


def _ln_mod_call(x2d, mod, rows_per_group, shift_row, scale_row, name):
    r, d = x2d.shape
    tr = 512
    tiles_per_group = rows_per_group // tr
    return pl.pallas_call(
        functools.partial(_ln_mod_kernel, shift_row=shift_row, scale_row=scale_row),
        out_shape=jax.ShapeDtypeStruct((r, d), _BF16),
        grid=(r // tr,),
        in_specs=[pl.BlockSpec((tr, d), lambda i: (i, 0)),
                  pl.BlockSpec((1, 6, d), lambda i: (i // tiles_per_group, 0, 0))],
        out_specs=pl.BlockSpec((tr, d), lambda i: (i, 0)),
        compiler_params=_params(("parallel",)),
        name=name,
    )(x2d, mod)


def _rope(xs, c, sa, sb):
    return xs * c + pltpu.roll(xs, 96, 1) * sa + pltpu.roll(xs, 32, 1) * sb


def _qkv_kernel(h_ref, w_ref, tab_ref, o_ref):
    j = pl.program_id(1)
    acc = jnp.dot(h_ref[...], w_ref[...], preferred_element_type=_F32)
    n_sl = acc.shape[1] // HEAD_DIM

    @pl.when(j < Q_COLS // acc.shape[1])
    def _():
        c, sa, sb = tab_ref[0], tab_ref[1], tab_ref[2]
        for hh in range(n_sl):
            sl = slice(hh * HEAD_DIM, (hh + 1) * HEAD_DIM)
            o_ref[:, sl] = _rope(acc[:, sl], c, sa, sb).astype(o_ref.dtype)

    @pl.when(j == Q_COLS // acc.shape[1])
    def _():
        c, sa, sb = tab_ref[3], tab_ref[4], tab_ref[5]
        for hh in range(N_KV_HEADS):
            sl = slice(hh * HEAD_DIM, (hh + 1) * HEAD_DIM)
            o_ref[:, sl] = _rope(acc[:, sl], c, sa, sb).astype(o_ref.dtype)
        o_ref[:, KV_COLS:] = acc[:, KV_COLS:].astype(o_ref.dtype)


def _qkv_call(h, w_in, tab, seq):
    m, d = h.shape
    tm, tn = 1024, 1024
    n_out = OFF_GLU
    t_blocks = seq // tm
    return pl.pallas_call(
        _qkv_kernel,
        out_shape=jax.ShapeDtypeStruct((m, n_out), _BF16),
        grid=(m // tm, n_out // tn),
        in_specs=[pl.BlockSpec((tm, d), lambda i, j: (i, 0)),
                  pl.BlockSpec((d, tn), lambda i, j: (0, j)),
                  pl.BlockSpec((6, tm, HEAD_DIM), lambda i, j: (0, i % t_blocks, 0))],
        out_specs=pl.BlockSpec((tm, tn), lambda i, j: (i, j)),
        compiler_params=_params(("parallel", "arbitrary")),
        name="proj_qkv",
    )(h, w_in, tab)


def _glu_kernel(h_ref, wa_ref, wg_ref, o_ref):
    h = h_ref[...]
    a = jnp.dot(h, wa_ref[...], preferred_element_type=_F32)
    g = jnp.dot(h, wg_ref[...], preferred_element_type=_F32)
    o_ref[...] = (a * jax.nn.sigmoid(g)).astype(o_ref.dtype)


def _glu_call(h, w_in):
    m, d = h.shape
    tm, tn = 1024, 1024
    a0 = OFF_GLU // tn
    g0 = (OFF_GLU + D_MODEL) // tn
    return pl.pallas_call(
        _glu_kernel,
        out_shape=jax.ShapeDtypeStruct((m, D_MODEL), _BF16),
        grid=(m // tm, D_MODEL // tn),
        in_specs=[pl.BlockSpec((tm, d), lambda i, j: (i, 0)),
                  pl.BlockSpec((d, tn), lambda i, j: (0, a0 + j)),
                  pl.BlockSpec((d, tn), lambda i, j: (0, g0 + j))],
        out_specs=pl.BlockSpec((tm, tn), lambda i, j: (i, j)),
        compiler_params=_params(("parallel", "arbitrary")),
        name="proj_glu",
    )(h, w_in, w_in)


def _gate_kernel(h_ref, w_ref, o_ref):
    acc = jnp.dot(h_ref[...], w_ref[...], preferred_element_type=_F32)
    o_ref[...] = jax.nn.sigmoid(acc).astype(o_ref.dtype)


def _gate_call(h, w_in):
    m, d = h.shape
    tm, tn = 1024, 1024
    c0 = OFF_GATE // tn
    return pl.pallas_call(
        _gate_kernel,
        out_shape=jax.ShapeDtypeStruct((m, 2 * D_MODEL), _BF16),
        grid=(m // tm, 2 * D_MODEL // tn),
        in_specs=[pl.BlockSpec((tm, d), lambda i, j: (i, 0)),
                  pl.BlockSpec((d, tn), lambda i, j: (0, c0 + j))],
        out_specs=pl.BlockSpec((tm, tn), lambda i, j: (i, j)),
        compiler_params=_params(("parallel", "arbitrary")),
        name="proj_gate",
    )(h, w_in)


def _matmul_kernel(a_ref, b_ref, o_ref):
    o_ref[...] = jnp.dot(a_ref[...], b_ref[...], preferred_element_type=_F32).astype(o_ref.dtype)


def _ctx_kv_call(hc, w_in):
    m, d = hc.shape
    tm, tn = 1024, 1024
    c0 = OFF_K // tn
    return pl.pallas_call(
        _matmul_kernel,
        out_shape=jax.ShapeDtypeStruct((m, 2 * KV_COLS), _BF16),
        grid=(m // tm,),
        in_specs=[pl.BlockSpec((tm, d), lambda i: (i, 0)),
                  pl.BlockSpec((d, tn), lambda i: (0, c0))],
        out_specs=pl.BlockSpec((tm, tn), lambda i: (i, 0)),
        compiler_params=_params(("parallel",)),
        name="ctx_kv",
    )(hc, w_in)


def _attn_kernel(sink_ref, q_ref, kp_ref, ko_ref, kn_ref, vp_ref, vo_ref, vn_ref, kc_ref, vc_ref, o_ref):
    i = pl.program_id(1)
    nb = pl.num_programs(1)
    rows = GROUP * BLOCK
    row = lax.broadcasted_iota(jnp.int32, (rows, BLOCK), 0)
    ii = jnp.bitwise_and(row, BLOCK - 1)
    jj = lax.broadcasted_iota(jnp.int32, (rows, BLOCK), 1)
    mask_p = jnp.logical_and(jj >= ii, i > 0)
    mask_n = jnp.logical_and(jj <= ii, i < nb - 1)
    row1 = lax.broadcasted_iota(jnp.int32, (rows, 1), 0)
    dn_t = (((1,), (1,)), ((), ()))
    for g in range(N_KV_HEADS):
        ks = slice(g * HEAD_DIM, (g + 1) * HEAD_DIM)
        q4 = jnp.concatenate(
            [q_ref[0, :, (g * GROUP + h) * HEAD_DIM:(g * GROUP + h + 1) * HEAD_DIM] for h in range(GROUP)], axis=0)
        k_all = jnp.concatenate([kp_ref[0, :, ks], ko_ref[0, :, ks], kn_ref[0, :, ks], kc_ref[0, :, ks]], axis=0)
        v_all = jnp.concatenate([vp_ref[0, :, ks], vo_ref[0, :, ks], vn_ref[0, :, ks], vc_ref[0, :, ks]], axis=0)
        s = lax.dot_general(q4, k_all, dn_t, preferred_element_type=_F32)
        s_p = jnp.where(mask_p, s[:, 0:BLOCK], NEG_INF)
        s_o = s[:, BLOCK:2 * BLOCK]
        s_n = jnp.where(mask_n, s[:, 2 * BLOCK:3 * BLOCK], NEG_INF)
        n_ctx = (s.shape[1] - 3 * BLOCK) // BLOCK
        s_c = [s[:, (3 + t) * BLOCK:(4 + t) * BLOCK] for t in range(n_ctx)]
        sink = jnp.full((rows, 1), sink_ref[g * GROUP + GROUP - 1], _F32)
        for h in range(GROUP - 2, -1, -1):
            sink = jnp.where(row1 < (h + 1) * BLOCK, sink_ref[g * GROUP + h], sink)
        mx = jnp.maximum(jnp.maximum(s_p, s_o), s_n)
        for t in s_c:
            mx = jnp.maximum(mx, t)
        m = jnp.maximum(jnp.max(mx, axis=-1, keepdims=True), sink)
        parts = [jnp.exp(t - m) for t in [s_p, s_o, s_n] + s_c]
        tot = parts[0]
        for t in parts[1:]:
            tot = tot + t
        denom = jnp.sum(tot, axis=-1, keepdims=True) + jnp.exp(sink - m)
        p = jnp.concatenate([t.astype(_BF16) for t in parts], axis=1)
        o = jnp.dot(p, v_all, preferred_element_type=_F32) / denom
        for h in range(GROUP):
            c0 = (g * GROUP + h) * HEAD_DIM
            o_ref[0, :, c0:c0 + HEAD_DIM] = o[h * BLOCK:(h + 1) * BLOCK, :].astype(o_ref.dtype)


def _attn_call(qkv, kvc, sink):
    b, s, _ = qkv.shape
    l = kvc.shape[1]
    nb = s // BLOCK
    kcol = OFF_K // KV_COLS
    vcol = OFF_V // KV_COLS
    kv_blk = (1, BLOCK, KV_COLS)
    return pl.pallas_call(
        _attn_kernel,
        out_shape=jax.ShapeDtypeStruct((b, s, Q_COLS), _BF16),
        grid=(b, nb),
        in_specs=[pl.BlockSpec(memory_space=pltpu.SMEM),
                  pl.BlockSpec((1, BLOCK, Q_COLS), lambda bi, i: (bi, i, 0)),
                  pl.BlockSpec(kv_blk, lambda bi, i: (bi, jnp.maximum(i - 1, 0), kcol)),
                  pl.BlockSpec(kv_blk, lambda bi, i: (bi, i, kcol)),
                  pl.BlockSpec(kv_blk, lambda bi, i: (bi, jnp.minimum(i + 1, nb - 1), kcol)),
                  pl.BlockSpec(kv_blk, lambda bi, i: (bi, jnp.maximum(i - 1, 0), vcol)),
                  pl.BlockSpec(kv_blk, lambda bi, i: (bi, i, vcol)),
                  pl.BlockSpec(kv_blk, lambda bi, i: (bi, jnp.minimum(i + 1, nb - 1), vcol)),
                  pl.BlockSpec((1, l, KV_COLS), lambda bi, i: (bi, 0, 0)),
                  pl.BlockSpec((1, l, KV_COLS), lambda bi, i: (bi, 0, 1))],
        out_specs=pl.BlockSpec((1, BLOCK, Q_COLS), lambda bi, i: (bi, i, 0)),
        compiler_params=_params(("parallel", "arbitrary")),
        name="attn",
    )(sink, qkv, qkv, qkv, qkv, qkv, qkv, qkv, kvc, kvc)


def _conv_kernel(u_ref, w_ref, b_ref, o_ref, pad_ref, win_ref, wb_ref, y_ref):
    s, ct = u_ref.shape[1], u_ref.shape[2]
    part = s // SUB
    span = part + 2 * CONV_HALO
    rc = 8
    pad_ref[0:CONV_HALO, :] = jnp.zeros((CONV_HALO, ct), _F32)
    pad_ref[CONV_HALO + s:, :] = jnp.zeros((pad_ref.shape[0] - CONV_HALO - s, ct), _F32)
    pad_ref[CONV_HALO:CONV_HALO + s, :] = u_ref[0].astype(_F32)
    first = CONV_HALO - CONV_K // 2
    n_lt = ct // LANES
    for p in range(SUB):
        for lt in range(n_lt):
            win_ref[lt, pl.ds(p, span, stride=SUB), :] = pad_ref[
                p * part + first:p * part + first + span, lt * LANES:(lt + 1) * LANES]
    for k in range(CONV_K):
        wb_ref[k * SUB:(k + 1) * SUB, :] = jnp.broadcast_to(w_ref[k:k + 1, :], (SUB, ct))
    bias = jnp.broadcast_to(b_ref[...], (SUB, ct))

    def body(c, carry):
        r0 = c * rc
        for lt in range(n_lt):
            ls = slice(lt * LANES, (lt + 1) * LANES)
            accs = [bias[:, ls] for _ in range(rc)]
            for k in range(CONV_K):
                wk = wb_ref[k * SUB:(k + 1) * SUB, ls]
                for j in range(rc):
                    q = pl.multiple_of((r0 + j + k) * SUB, SUB)
                    accs[j] = accs[j] + win_ref[lt, pl.ds(q, SUB), :] * wk
            for j in range(rc):
                y_ref[lt, pl.ds(pl.multiple_of((r0 + j) * SUB, SUB), SUB), :] = accs[j]
        return carry

    lax.fori_loop(0, part // rc, body, 0)
    for p in range(SUB):
        for lt in range(n_lt):
            o_ref[0, p * part:(p + 1) * part, lt * LANES:(lt + 1) * LANES] = y_ref[lt, pl.ds(p, part, stride=SUB), :]


def _conv_call(u, conv_w, conv_b):
    b, s, c = u.shape
    ct = 256
    part = s // SUB
    return pl.pallas_call(
        _conv_kernel,
        out_shape=jax.ShapeDtypeStruct((b, s, c), _F32),
        grid=(b, c // ct),
        in_specs=[pl.BlockSpec((1, s, ct), lambda bi, j: (bi, 0, j)),
                  pl.BlockSpec((CONV_K, ct), lambda bi, j: (0, j)),
                  pl.BlockSpec((1, ct), lambda bi, j: (0, j))],
        out_specs=pl.BlockSpec((1, s, ct), lambda bi, j: (bi, 0, j)),
        scratch_shapes=[pltpu.VMEM((s + 2 * CONV_HALO + SUB, ct), _F32),
                        pltpu.VMEM((ct // LANES, (part + 2 * CONV_HALO) * SUB, LANES), _F32),
                        pltpu.VMEM((CONV_K * SUB, ct), _F32),
                        pltpu.VMEM((ct // LANES, s, LANES), _F32)],
        compiler_params=_params(("parallel", "arbitrary")),
        name="conv",
    )(u, conv_w, conv_b)


def _merge_kernel(attn_ref, y_ref, ga_ref, gc_ref, ng_ref, nb_ref, wa_ref, wc_ref, o_ref, yc_ref):
    @pl.when(pl.program_id(1) == 0)
    def _():
        z = _ln(y_ref[...]) * ng_ref[...] + nb_ref[...]
        yc_ref[...] = (z * jax.nn.sigmoid(z)).astype(yc_ref.dtype)

    ya = jnp.dot(attn_ref[...], wa_ref[...], preferred_element_type=_F32)
    yc = jnp.dot(yc_ref[...], wc_ref[...], preferred_element_type=_F32)
    o_ref[...] = (ga_ref[...].astype(_F32) * ya + gc_ref[...].astype(_F32) * yc).astype(o_ref.dtype)


def _merge_call(attn, y, gates, norm_g, norm_b, w_attn, w_conv):
    m, d = attn.shape
    tm, tn = 512, 1024
    nj = d // tn
    return pl.pallas_call(
        _merge_kernel,
        out_shape=jax.ShapeDtypeStruct((m, d), _BF16),
        grid=(m // tm, nj),
        in_specs=[pl.BlockSpec((tm, d), lambda i, j: (i, 0)),
                  pl.BlockSpec((tm, d), lambda i, j: (i, 0)),
                  pl.BlockSpec((tm, tn), lambda i, j: (i, j)),
                  pl.BlockSpec((tm, tn), lambda i, j: (i, nj + j)),
                  pl.BlockSpec((1, d), lambda i, j: (0, 0)),
                  pl.BlockSpec((1, d), lambda i, j: (0, 0)),
                  pl.BlockSpec((d, tn), lambda i, j: (0, j)),
                  pl.BlockSpec((d, tn), lambda i, j: (0, j))],
        out_specs=pl.BlockSpec((tm, tn), lambda i, j: (i, j)),
        scratch_shapes=[pltpu.VMEM((tm, d), _BF16)],
        compiler_params=_params(("parallel", "arbitrary")),
        name="merge",
    )(attn, y, gates, gates, norm_g, norm_b, w_attn, w_conv)


def _outproj_kernel(m_ref, x_ref, mod_ref, w_ref, g_ref, b_ref, x1_ref, h2_ref):
    mix = jnp.dot(m_ref[...], w_ref[...], preferred_element_type=_F32)
    gate1 = mod_ref[0, 2:3, :]
    shift2 = mod_ref[0, 3:4, :]
    scale2 = mod_ref[0, 4:5, :]
    r = ALPHA * _ln(x_ref[...]) + gate1 * mix
    x1 = _ln(r) * g_ref[...] + b_ref[...]
    x1_ref[...] = x1
    h2_ref[...] = (x1 * (1.0 + scale2) + shift2).astype(h2_ref.dtype)


def _outproj_call(mixed, x2d, mod, w_out, ln_g, ln_b, seq):
    m, d = mixed.shape
    tm = 512
    tiles_per_batch = seq // tm
    return pl.pallas_call(
        _outproj_kernel,
        out_shape=(jax.ShapeDtypeStruct((m, d), _F32), jax.ShapeDtypeStruct((m, d), _BF16)),
        grid=(m // tm,),
        in_specs=[pl.BlockSpec((tm, d), lambda i: (i, 0)),
                  pl.BlockSpec((tm, d), lambda i: (i, 0)),
                  pl.BlockSpec((1, 6, d), lambda i: (i // tiles_per_batch, 0, 0)),
                  pl.BlockSpec((d, d), lambda i: (0, 0)),
                  pl.BlockSpec((1, d), lambda i: (0, 0)),
                  pl.BlockSpec((1, d), lambda i: (0, 0))],
        out_specs=(pl.BlockSpec((tm, d), lambda i: (i, 0)),
                   pl.BlockSpec((tm, d), lambda i: (i, 0))),
        compiler_params=_params(("parallel",)),
        name="outproj",
    )(mixed, x2d, mod, w_out, ln_g, ln_b)


def _ffn_kernel(h_ref, x1_ref, mod_ref, wg_ref, wu_ref, wo_ref, g_ref, b_ref, o_ref, acc_ref):
    k = pl.program_id(1)

    @pl.when(k == 0)
    def _():
        acc_ref[...] = jnp.zeros_like(acc_ref)

    h = h_ref[...]
    g = jnp.dot(h, wg_ref[...], preferred_element_type=_F32)
    u = jnp.dot(h, wu_ref[...], preferred_element_type=_F32)
    a = (g * jax.nn.sigmoid(g) * u).astype(_BF16)
    acc_ref[...] += jnp.dot(a, wo_ref[...], preferred_element_type=_F32)

    @pl.when(k == pl.num_programs(1) - 1)
    def _():
        gate2 = mod_ref[0, 5:6, :]
        r = ALPHA * x1_ref[...] + gate2 * acc_ref[...]
        o_ref[...] = _ln(r) * g_ref[...] + b_ref[...]


def _ffn_call(h2, x1, mod, w_ffn_in, w_ffn_out, ln_g, ln_b, seq):
    m, d = h2.shape
    hidden = w_ffn_out.shape[0]
    tm, tk = 512, 512
    nk = hidden // tk
    tiles_per_batch = seq // tm
    return pl.pallas_call(
        _ffn_kernel,
        out_shape=jax.ShapeDtypeStruct((m, d), _F32),
        grid=(m // tm, nk),
        in_specs=[pl.BlockSpec((tm, d), lambda i, k: (i, 0)),
                  pl.BlockSpec((tm, d), lambda i, k: (i, 0)),
                  pl.BlockSpec((1, 6, d), lambda i, k: (i // tiles_per_batch, 0, 0)),
                  pl.BlockSpec((d, tk), lambda i, k: (0, k)),
                  pl.BlockSpec((d, tk), lambda i, k: (0, nk + k)),
                  pl.BlockSpec((tk, d), lambda i, k: (k, 0)),
                  pl.BlockSpec((1, d), lambda i, k: (0, 0)),
                  pl.BlockSpec((1, d), lambda i, k: (0, 0))],
        out_specs=pl.BlockSpec((tm, d), lambda i, k: (i, 0)),
        scratch_shapes=[pltpu.VMEM((tm, d), _F32)],
        compiler_params=_params(("parallel", "arbitrary")),
        name="ffn",
    )(h2, x1, mod, w_ffn_in, w_ffn_in, w_ffn_out, ln_g, ln_b)


def _rope_tables(seq):
    t = jnp.arange(seq, dtype=jnp.int32)
    n_freq = HEAD_DIM // 4
    inv_freq = ROPE_BASE ** (-jnp.arange(n_freq, dtype=_F32) / n_freq)
    ang_r = (t // GRID_W).astype(_F32)[:, None] * inv_freq
    ang_c = (t % GRID_W).astype(_F32)[:, None] * inv_freq
    zero = jnp.zeros_like(ang_r)
    cos = jnp.concatenate([jnp.cos(ang_r), jnp.cos(ang_r), jnp.cos(ang_c), jnp.cos(ang_c)], axis=1)
    sa = jnp.concatenate([-jnp.sin(ang_r), zero, -jnp.sin(ang_c), zero], axis=1)
    sb = jnp.concatenate([zero, jnp.sin(ang_r), zero, jnp.sin(ang_c)], axis=1)
    base = jnp.stack([cos, sa, sb])
    return jnp.concatenate([base * SCALE, base], axis=0)


def kernel(x, c, ctx, c_ctx, w_mod, b_mod, w_in, attn_sink, conv_w, conv_b, conv_norm_g, conv_norm_b,
           w_attn_proj, w_conv_proj, w_out, ln1_g, ln1_b, w_ffn_in, w_ffn_out, ln2_g, ln2_b):
    b, s, d = x.shape
    l = ctx.shape[1]
    m = b * s

    c_all = jnp.concatenate([c, c_ctx[None, :], jnp.zeros((16 - b - 1, d), _F32)], axis=0)
    mod_all = _mod_call(c_all, w_mod[0], b_mod[0][None, :])
    mod = mod_all[:b].reshape(b, 6, d)
    mod_c = mod_all[b:b + 1].reshape(1, 6, d)

    w_in_b = w_in[0].astype(_BF16)
    x2d = x.reshape(m, d)
    h = _ln_mod_call(x2d, mod, s, 0, 1, "ln_mod_x")
    hc = _ln_mod_call(ctx.reshape(b * l, d), mod_c, b * l, 0, 1, "ln_mod_ctx")

    tab = _rope_tables(s)
    qkv = _qkv_call(h, w_in_b, tab, s).reshape(b, s, OFF_GLU)
    kvc = _ctx_kv_call(hc, w_in_b).reshape(b, l, 2 * KV_COLS)
    u = _glu_call(h, w_in_b).reshape(b, s, d)
    gates = _gate_call(h, w_in_b)

    attn = _attn_call(qkv, kvc, attn_sink[0]).reshape(m, d)
    y = _conv_call(u, conv_w[0], conv_b[0][None, :]).reshape(m, d)

    mixed = _merge_call(attn, y, gates, conv_norm_g[0][None, :], conv_norm_b[0][None, :],
                        w_attn_proj[0].astype(_BF16), w_conv_proj[0].astype(_BF16))
    x1, h2 = _outproj_call(mixed, x2d, mod, w_out[0].astype(_BF16), ln1_g[0][None, :], ln1_b[0][None, :], s)
    out = _ffn_call(h2, x1, mod, w_ffn_in[0].astype(_BF16), w_ffn_out[0].astype(_BF16),
                    ln2_g[0][None, :], ln2_b[0][None, :], s)
    return out.reshape(b, s, d)
```

```python
import functools

import jax
import jax.numpy as jnp
from jax import lax
from jax.experimental import pallas as pl
from jax.experimental.pallas import tpu as pltpu

D_MODEL = 2048
GRID_W = 64
HEAD_DIM = 128
N_HEADS = 16
N_KV_HEADS = 4
GROUP = N_HEADS // N_KV_HEADS
BLOCK = 128
CONV_K = 31
CONV_HALO = 16
SUB = 8
LANES = 128
ROPE_BASE = 10000.0
DEPTH = 1
ALPHA = (2.0 * DEPTH) ** 0.25
EPS = 1e-6
NEG_INF = -1e30
SCALE = HEAD_DIM ** -0.5

Q_COLS = N_HEADS * HEAD_DIM
KV_COLS = N_KV_HEADS * HEAD_DIM
OFF_K = Q_COLS
OFF_V = OFF_K + KV_COLS
OFF_GLU = OFF_V + KV_COLS
OFF_GATE = OFF_GLU + 2 * D_MODEL

VMEM_LIMIT = 56 * 1024 * 1024

_BF16 = jnp.bfloat16
_F32 = jnp.float32


def _params(sem):
    return pltpu.CompilerParams(dimension_semantics=sem, vmem_limit_bytes=VMEM_LIMIT)


def _ln(xf):
    mu = jnp.mean(xf, axis=-1, keepdims=True)
    xc = xf - mu
    var = jnp.mean(xc * xc, axis=-1, keepdims=True)
    return xc * lax.rsqrt(var + EPS)


def _mod_kernel(c_ref, w_ref, b_ref, o_ref):
    c = c_ref[...]
    s = (c * jax.nn.sigmoid(c)).astype(_BF16)
    o_ref[...] = jnp.dot(s, w_ref[...].astype(_BF16), preferred_element_type=_F32) + b_ref[...]


def _mod_call(c_all, w_mod, b_mod):
    rows, d = c_all.shape
    n = w_mod.shape[1]
    tn = 1024
    return pl.pallas_call(
        _mod_kernel,
        out_shape=jax.ShapeDtypeStruct((rows, n), _F32),
        grid=(n // tn,),
        in_specs=[pl.BlockSpec((rows, d), lambda j: (0, 0)),
                  pl.BlockSpec((d, tn), lambda j: (0, j)),
                  pl.BlockSpec((1, tn), lambda j: (0, j))],
        out_specs=pl.BlockSpec((rows, tn), lambda j: (0, j)),
        compiler_params=_params(("arbitrary",)),
        name="mod",
    )(c_all, w_mod, b_mod)


def _ln_mod_kernel(x_ref, mod_ref, o_ref, *, shift_row, scale_row):
    y = _ln(x_ref[...])
    shift = mod_ref[0, shift_row:shift_row + 1, :]
    scale = mod_ref[0, scale_row:scale_row + 1, :]
    o_ref[...] = (y * (1.0 + scale) + shift).astype(o_ref.dtype)


def _ln_mod_call(x2d, mod, rows_per_group, shift_row, scale_row, name):
    r, d = x2d.shape
    tr = 512
    tiles_per_group = rows_per_group // tr
    return pl.pallas_call(
        functools.partial(_ln_mod_kernel, shift_row=shift_row, scale_row=scale_row),
        out_shape=jax.ShapeDtypeStruct((r, d), _BF16),
        grid=(r // tr,),
        in_specs=[pl.BlockSpec((tr, d), lambda i: (i, 0)),
                  pl.BlockSpec((1, 6, d), lambda i: (i // tiles_per_group, 0, 0))],
        out_specs=pl.BlockSpec((tr, d), lambda i: (i, 0)),
        compiler_params=_params(("parallel",)),
        name=name,
    )(x2d, mod)


def _rope(xs, c, sa, sb):
    return xs * c + pltpu.roll(xs, 96, 1) * sa + pltpu.roll(xs, 32, 1) * sb


def _qkv_kernel(h_ref, w_ref, tab_lo_ref, tab_hi_ref, o_ref):
    acc = jnp.dot(h_ref[...], w_ref[...], preferred_element_type=_F32)
    n_sl = acc.shape[1] // HEAD_DIM
    for hh in range(n_sl):
        tab_ref = tab_lo_ref if hh < n_sl // 2 else tab_hi_ref
        sl = slice(hh * HEAD_DIM, (hh + 1) * HEAD_DIM)
        o_ref[:, sl] = _rope(acc[:, sl], tab_ref[0, 0], tab_ref[0, 1], tab_ref[0, 2]).astype(o_ref.dtype)


def _qkv_call(h, w_in, tab, seq):
    m, d = h.shape
    tm, tn = 1024, 1024
    n_out = OFF_GLU
    t_blocks = seq // tm
    q_tiles = Q_COLS // tn
    tab_blk = (1, 3, tm, HEAD_DIM)
    return pl.pallas_call(
        _qkv_kernel,
        out_shape=jax.ShapeDtypeStruct((m, n_out), _BF16),
        grid=(m // tm, n_out // tn),
        in_specs=[pl.BlockSpec((tm, d), lambda i, j: (i, 0)),
                  pl.BlockSpec((d, tn), lambda i, j: (0, j)),
                  pl.BlockSpec(tab_blk, lambda i, j: (jnp.where(j < q_tiles, 0, 1), 0, i % t_blocks, 0)),
                  pl.BlockSpec(tab_blk, lambda i, j: (jnp.where(j < q_tiles, 0, 2), 0, i % t_blocks, 0))],
        out_specs=pl.BlockSpec((tm, tn), lambda i, j: (i, j)),
        compiler_params=_params(("parallel", "arbitrary")),
        name="proj_qkv",
    )(h, w_in, tab, tab)


def _glu_kernel(h_ref, wa_ref, wg_ref, o_ref):
    h = h_ref[...]
    a = jnp.dot(h, wa_ref[...], preferred_element_type=_F32)
    g = jnp.dot(h, wg_ref[...], preferred_element_type=_F32)
    o_ref[...] = (a * jax.nn.sigmoid(g)).astype(o_ref.dtype)


def _glu_call(h, w_in):
    m, d = h.shape
    tm, tn = 1024, 1024
    a0 = OFF_GLU // tn
    g0 = (OFF_GLU + D_MODEL) // tn
    return pl.pallas_call(
        _glu_kernel,
        out_shape=jax.ShapeDtypeStruct((m, D_MODEL), _BF16),
        grid=(m // tm, D_MODEL // tn),
        in_specs=[pl.BlockSpec((tm, d), lambda i, j: (i, 0)),
                  pl.BlockSpec((d, tn), lambda i, j: (0, a0 + j)),
                  pl.BlockSpec((d, tn), lambda i, j: (0, g0 + j))],
        out_specs=pl.BlockSpec((tm, tn), lambda i, j: (i, j)),
        compiler_params=_params(("parallel", "arbitrary")),
        name="proj_glu",
    )(h, w_in, w_in)


def _gate_kernel(h_ref, w_ref, o_ref):
    acc = jnp.dot(h_ref[...], w_ref[...], preferred_element_type=_F32)
    o_ref[...] = jax.nn.sigmoid(acc).astype(o_ref.dtype)


def _gate_call(h, w_in):
    m, d = h.shape
    tm, tn = 1024, 1024
    c0 = OFF_GATE // tn
    return pl.pallas_call(
        _gate_kernel,
        out_shape=jax.ShapeDtypeStruct((m, 2 * D_MODEL), _BF16),
        grid=(m // tm, 2 * D_MODEL // tn),
        in_specs=[pl.BlockSpec((tm, d), lambda i, j: (i, 0)),
                  pl.BlockSpec((d, tn), lambda i, j: (0, c0 + j))],
        out_specs=pl.BlockSpec((tm, tn), lambda i, j: (i, j)),
        compiler_params=_params(("parallel", "arbitrary")),
        name="proj_gate",
    )(h, w_in)


def _matmul_kernel(a_ref, b_ref, o_ref):
    o_ref[...] = jnp.dot(a_ref[...], b_ref[...], preferred_element_type=_F32).astype(o_ref.dtype)


def _ctx_kv_call(hc, w_in):
    m, d = hc.shape
    tm, tn = 1024, 1024
    c0 = OFF_K // tn
    return pl.pallas_call(
        _matmul_kernel,
        out_shape=jax.ShapeDtypeStruct((m, 2 * KV_COLS), _BF16),
        grid=(m // tm,),
        in_specs=[pl.BlockSpec((tm, d), lambda i: (i, 0)),
                  pl.BlockSpec((d, tn), lambda i: (0, c0))],
        out_specs=pl.BlockSpec((tm, tn), lambda i: (i, 0)),
        compiler_params=_params(("parallel",)),
        name="ctx_kv",
    )(hc, w_in)


def _attn_kernel(sink_ref, q_ref, kp_ref, ko_ref, kn_ref, vp_ref, vo_ref, vn_ref, kc_ref, vc_ref, o_ref):
    i = pl.program_id(1)
    nb = pl.num_programs(1)
    rows = GROUP * BLOCK
    row = lax.broadcasted_iota(jnp.int32, (rows, BLOCK), 0)
    ii = jnp.bitwise_and(row, BLOCK - 1)
    jj = lax.broadcasted_iota(jnp.int32, (rows, BLOCK), 1)
    mask_p = jnp.logical_and(jj >= ii, i > 0)
    mask_n = jnp.logical_and(jj <= ii, i < nb - 1)
    row1 = lax.broadcasted_iota(jnp.int32, (rows, 1), 0)
    dn_t = (((1,), (1,)), ((), ()))
    for g in range(N_KV_HEADS):
        ks = slice(g * HEAD_DIM, (g + 1) * HEAD_DIM)
        q4 = jnp.concatenate(
            [q_ref[0, :, (g * GROUP + h) * HEAD_DIM:(g * GROUP + h + 1) * HEAD_DIM] for h in range(GROUP)], axis=0)
        k_all = jnp.concatenate([kp_ref[0, :, ks], ko_ref[0, :, ks], kn_ref[0, :, ks], kc_ref[0, :, ks]], axis=0)
        v_all = jnp.concatenate([vp_ref[0, :, ks], vo_ref[0, :, ks], vn_ref[0, :, ks], vc_ref[0, :, ks]], axis=0)
        s = lax.dot_general(q4, k_all, dn_t, preferred_element_type=_F32)
        s_p = jnp.where(mask_p, s[:, 0:BLOCK], NEG_INF)
        s_o = s[:, BLOCK:2 * BLOCK]
        s_n = jnp.where(mask_n, s[:, 2 * BLOCK:3 * BLOCK], NEG_INF)
        n_ctx = (s.shape[1] - 3 * BLOCK) // BLOCK
        s_c = [s[:, (3 + t) * BLOCK:(4 + t) * BLOCK] for t in range(n_ctx)]
        sink = jnp.full((rows, 1), sink_ref[g * GROUP + GROUP - 1], _F32)
        for h in range(GROUP - 2, -1, -1):
            sink = jnp.where(row1 < (h + 1) * BLOCK, sink_ref[g * GROUP + h], sink)
        mx = jnp.maximum(jnp.maximum(s_p, s_o), s_n)
        for t in s_c:
            mx = jnp.maximum(mx, t)
        m = jnp.maximum(jnp.max(mx, axis=-1, keepdims=True), sink)
        parts = [jnp.exp(t - m) for t in [s_p, s_o, s_n] + s_c]
        tot = parts[0]
        for t in parts[1:]:
            tot = tot + t
        denom = jnp.sum(tot, axis=-1, keepdims=True) + jnp.exp(sink - m)
        p = jnp.concatenate([t.astype(_BF16) for t in parts], axis=1)
        o = jnp.dot(p, v_all, preferred_element_type=_F32) / denom
        for h in range(GROUP):
            c0 = (g * GROUP + h) * HEAD_DIM
            o_ref[0, :, c0:c0 + HEAD_DIM] = o[h * BLOCK:(h + 1) * BLOCK, :].astype(o_ref.dtype)


def _attn_call(qkv, kvc, sink):
    b, s, _ = qkv.shape
    l = kvc.shape[1]
    nb = s // BLOCK
    kcol = OFF_K // KV_COLS
    vcol = OFF_V // KV_COLS
    kv_blk = (1, BLOCK, KV_COLS)
    return pl.pallas_call(
        _attn_kernel,
        out_shape=jax.ShapeDtypeStruct((b, s, Q_COLS), _BF16),
        grid=(b, nb),
        in_specs=[pl.BlockSpec(memory_space=pltpu.SMEM),
                  pl.BlockSpec((1, BLOCK, Q_COLS), lambda bi, i: (bi, i, 0)),
                  pl.BlockSpec(kv_blk, lambda bi, i: (bi, jnp.maximum(i - 1, 0), kcol)),
                  pl.BlockSpec(kv_blk, lambda bi, i: (bi, i, kcol)),
                  pl.BlockSpec(kv_blk, lambda bi, i: (bi, jnp.minimum(i + 1, nb - 1), kcol)),
                  pl.BlockSpec(kv_blk, lambda bi, i: (bi, jnp.maximum(i - 1, 0), vcol)),
                  pl.BlockSpec(kv_blk, lambda bi, i: (bi, i, vcol)),
                  pl.BlockSpec(kv_blk, lambda bi, i: (bi, jnp.minimum(i + 1, nb - 1), vcol)),
                  pl.BlockSpec((1, l, KV_COLS), lambda bi, i: (bi, 0, 0)),
                  pl.BlockSpec((1, l, KV_COLS), lambda bi, i: (bi, 0, 1))],
        out_specs=pl.BlockSpec((1, BLOCK, Q_COLS), lambda bi, i: (bi, i, 0)),
        compiler_params=_params(("parallel", "arbitrary")),
        name="attn",
    )(sink, qkv, qkv, qkv, qkv, qkv, qkv, qkv, kvc, kvc)


def _conv_kernel(u_ref, w_ref, b_ref, o_ref, pad_ref, win_ref, wb_ref, y_ref):
    s, ct = u_ref.shape[1], u_ref.shape[2]
    part = s // SUB
    span = part + 2 * CONV_HALO
    rc = 8
    pad_ref[0:CONV_HALO, :] = jnp.zeros((CONV_HALO, ct), _F32)
    pad_ref[CONV_HALO + s:, :] = jnp.zeros((pad_ref.shape[0] - CONV_HALO - s, ct), _F32)
    pad_ref[CONV_HALO:CONV_HALO + s, :] = u_ref[0].astype(_F32)
    first = CONV_HALO - CONV_K // 2
    n_lt = ct // LANES
    for p in range(SUB):
        for lt in range(n_lt):
            win_ref[lt, pl.ds(p, span, stride=SUB), :] = pad_ref[
                p * part + first:p * part + first + span, lt * LANES:(lt + 1) * LANES]
    for k in range(CONV_K):
        wb_ref[k * SUB:(k + 1) * SUB, :] = jnp.broadcast_to(w_ref[k:k + 1, :], (SUB, ct))
    bias = jnp.broadcast_to(b_ref[...], (SUB, ct))

    def body(c, carry):
        r0 = c * rc
        for lt in range(n_lt):
            ls = slice(lt * LANES, (lt + 1) * LANES)
            accs = [bias[:, ls] for _ in range(rc)]
            for k in range(CONV_K):
                wk = wb_ref[k * SUB:(k + 1) * SUB, ls]
                for j in range(rc):
                    q = pl.multiple_of((r0 + j + k) * SUB, SUB)
                    accs[j] = accs[j] + win_ref[lt, pl.ds(q, SUB), :] * wk
            for j in range(rc):
                y_ref[lt, pl.ds(pl.multiple_of((r0 + j) * SUB, SUB), SUB), :] = accs[j]
        return carry

    lax.fori_loop(0, part // rc, body, 0)
    for p in range(SUB):
        for lt in range(n_lt):
            o_ref[0, p * part:(p + 1) * part, lt * LANES:(lt + 1) * LANES] = y_ref[lt, pl.ds(p, part, stride=SUB), :]


def _conv_call(u, conv_w, conv_b):
    b, s, c = u.shape
    ct = 256
    part = s // SUB
    return pl.pallas_call(
        _conv_kernel,
        out_shape=jax.ShapeDtypeStruct((b, s, c), _F32),
        grid=(b, c // ct),
        in_specs=[pl.BlockSpec((1, s, ct), lambda bi, j: (bi, 0, j)),
                  pl.BlockSpec((CONV_K, ct), lambda bi, j: (0, j)),
                  pl.BlockSpec((1, ct), lambda bi, j: (0, j))],
        out_specs=pl.BlockSpec((1, s, ct), lambda bi, j: (bi, 0, j)),
        scratch_shapes=[pltpu.VMEM((s + 2 * CONV_HALO + SUB, ct), _F32),
                        pltpu.VMEM((ct // LANES, (part + 2 * CONV_HALO) * SUB, LANES), _F32),
                        pltpu.VMEM((CONV_K * SUB, ct), _F32),
                        pltpu.VMEM((ct // LANES, s, LANES), _F32)],
        compiler_params=_params(("parallel", "arbitrary")),
        name="conv",
    )(u, conv_w, conv_b)


def _ln_silu_kernel(y_ref, g_ref, b_ref, o_ref):
    z = _ln(y_ref[...]) * g_ref[...] + b_ref[...]
    o_ref[...] = (z * jax.nn.sigmoid(z)).astype(o_ref.dtype)


def _ln_silu_call(y, norm_g, norm_b):
    m, d = y.shape
    tr = 512
    return pl.pallas_call(
        _ln_silu_kernel,
        out_shape=jax.ShapeDtypeStruct((m, d), _BF16),
        grid=(m // tr,),
        in_specs=[pl.BlockSpec((tr, d), lambda i: (i, 0)),
                  pl.BlockSpec((1, d), lambda i: (0, 0)),
                  pl.BlockSpec((1, d), lambda i: (0, 0))],
        out_specs=pl.BlockSpec((tr, d), lambda i: (i, 0)),
        compiler_params=_params(("parallel",)),
        name="ln_silu",
    )(y, norm_g, norm_b)


def _merge_kernel(attn_ref, yc_ref, ga_ref, gc_ref, wa_ref, wc_ref, o_ref):
    ya = jnp.dot(attn_ref[...], wa_ref[...], preferred_element_type=_F32)
    yc = jnp.dot(yc_ref[...], wc_ref[...], preferred_element_type=_F32)
    o_ref[...] = (ga_ref[...].astype(_F32) * ya + gc_ref[...].astype(_F32) * yc).astype(o_ref.dtype)


def _merge_call(attn, yc, gates, w_attn, w_conv):
    m, d = attn.shape
    tm, tn = 1024, 1024
    nj = d // tn
    return pl.pallas_call(
        _merge_kernel,
        out_shape=jax.ShapeDtypeStruct((m, d), _BF16),
        grid=(m // tm, nj),
        in_specs=[pl.BlockSpec((tm, d), lambda i, j: (i, 0)),
                  pl.BlockSpec((tm, d), lambda i, j: (i, 0)),
                  pl.BlockSpec((tm, tn), lambda i, j: (i, j)),
                  pl.BlockSpec((tm, tn), lambda i, j: (i, nj + j)),
                  pl.BlockSpec((d, tn), lambda i, j: (0, j)),
                  pl.BlockSpec((d, tn), lambda i, j: (0, j))],
        out_specs=pl.BlockSpec((tm, tn), lambda i, j: (i, j)),
        compiler_params=_params(("parallel", "arbitrary")),
        name="merge",
    )(attn, yc, gates, gates, w_attn, w_conv)


def _outproj_kernel(m_ref, x_ref, mod_ref, w_ref, g_ref, b_ref, x1_ref, h2_ref, mix_even_ref, mix_odd_ref):
    i = pl.program_id(0)

    @pl.when(i == 0)
    def _():
        mix_odd_ref[...] = jnp.zeros(mix_odd_ref.shape, _F32)

    def step(cur_ref, prev_ref):
        mix = prev_ref[...]
        cur_ref[...] = jnp.dot(m_ref[...], w_ref[...], preferred_element_type=_F32)
        gate1 = mod_ref[0, 2:3, :]
        shift2 = mod_ref[0, 3:4, :]
        scale2 = mod_ref[0, 4:5, :]
        r = ALPHA * _ln(x_ref[...]) + gate1 * mix
        x1 = _ln(r) * g_ref[...] + b_ref[...]
        x1_ref[...] = x1
        h2_ref[...] = (x1 * (1.0 + scale2) + shift2).astype(h2_ref.dtype)

    @pl.when(i % 2 == 0)
    def _():
        step(mix_even_ref, mix_odd_ref)

    @pl.when(i % 2 == 1)
    def _():
        step(mix_odd_ref, mix_even_ref)


def _outproj_call(mixed, x2d, mod, w_out, ln_g, ln_b, seq):
    m, d = mixed.shape
    tm = 512
    n = m // tm
    tiles_per_batch = seq // tm

    def prev(i):
        return jnp.maximum(i - 1, 0)

    return pl.pallas_call(
        _outproj_kernel,
        out_shape=(jax.ShapeDtypeStruct((m, d), _F32), jax.ShapeDtypeStruct((m, d), _BF16)),
        grid=(n + 1,),
        in_specs=[pl.BlockSpec((tm, d), lambda i: (jnp.minimum(i, n - 1), 0)),
                  pl.BlockSpec((tm, d), lambda i: (prev(i), 0)),
                  pl.BlockSpec((1, 6, d), lambda i: (prev(i) // tiles_per_batch, 0, 0)),
                  pl.BlockSpec((d, d), lambda i: (0, 0), pipeline_mode=pl.Buffered(1)),
                  pl.BlockSpec((1, d), lambda i: (0, 0)),
                  pl.BlockSpec((1, d), lambda i: (0, 0))],
        out_specs=(pl.BlockSpec((tm, d), lambda i: (prev(i), 0)),
                   pl.BlockSpec((tm, d), lambda i: (prev(i), 0))),
        scratch_shapes=[pltpu.VMEM((tm, d), _F32), pltpu.VMEM((tm, d), _F32)],
        compiler_params=_params(("arbitrary",)),
        name="outproj",
    )(mixed, x2d, mod, w_out, ln_g, ln_b)


def _ffn_kernel(h_ref, x1_ref, mod_ref, wg_ref, wu_ref, wo_ref, g_ref, b_ref, o_ref):
    k = pl.program_id(1)

    @pl.when(k == 0)
    def _():
        o_ref[...] = jnp.zeros_like(o_ref)

    h = h_ref[...]
    g = jnp.dot(h, wg_ref[...], preferred_element_type=_F32)
    u = jnp.dot(h, wu_ref[...], preferred_element_type=_F32)
    a = (g * jax.nn.sigmoid(g) * u).astype(_BF16)
    nc = 512
    for n in range(o_ref.shape[1] // nc):
        sl = slice(n * nc, (n + 1) * nc)
        o_ref[:, sl] += jnp.dot(a, wo_ref[:, sl], preferred_element_type=_F32)

    @pl.when(k == pl.num_programs(1) - 1)
    def _():
        gate2 = mod_ref[0, 5:6, :]
        r = ALPHA * x1_ref[...] + gate2 * o_ref[...]
        o_ref[...] = _ln(r) * g_ref[...] + b_ref[...]


def _ffn_call(h2, x1, mod, w_ffn_in, w_ffn_out, ln_g, ln_b, seq):
    m, d = h2.shape
    hidden = w_ffn_out.shape[0]
    tm, tk = 1024, 256
    nk = hidden // tk
    tiles_per_batch = seq // tm
    return pl.pallas_call(
        _ffn_kernel,
        out_shape=jax.ShapeDtypeStruct((m, d), _F32),
        grid=(m // tm, nk),
        in_specs=[pl.BlockSpec((tm, d), lambda i, k: (i, 0)),
                  pl.BlockSpec((tm, d), lambda i, k: (i, 0), pipeline_mode=pl.Buffered(1)),
                  pl.BlockSpec((1, 6, d), lambda i, k: (i // tiles_per_batch, 0, 0)),
                  pl.BlockSpec((d, tk), lambda i, k: (0, k)),
                  pl.BlockSpec((d, tk), lambda i, k: (0, nk + k)),
                  pl.BlockSpec((tk, d), lambda i, k: (k, 0)),
                  pl.BlockSpec((1, d), lambda i, k: (0, 0)),
                  pl.BlockSpec((1, d), lambda i, k: (0, 0))],
        out_specs=pl.BlockSpec((tm, d), lambda i, k: (i, 0)),
        compiler_params=_params(("parallel", "arbitrary")),
        name="ffn",
    )(h2, x1, mod, w_ffn_in, w_ffn_in, w_ffn_out, ln_g, ln_b)


def _rope_tables(seq):
    t = jnp.arange(seq, dtype=jnp.int32)
    n_freq = HEAD_DIM // 4
    inv_freq = ROPE_BASE ** (-jnp.arange(n_freq, dtype=_F32) / n_freq)
    ang_r = (t // GRID_W).astype(_F32)[:, None] * inv_freq
    ang_c = (t % GRID_W).astype(_F32)[:, None] * inv_freq
    zero = jnp.zeros_like(ang_r)
    cos = jnp.concatenate([jnp.cos(ang_r), jnp.cos(ang_r), jnp.cos(ang_c), jnp.cos(ang_c)], axis=1)
    sa = jnp.concatenate([-jnp.sin(ang_r), zero, -jnp.sin(ang_c), zero], axis=1)
    sb = jnp.concatenate([zero, jnp.sin(ang_r), zero, jnp.sin(ang_c)], axis=1)
    base = jnp.stack([cos, sa, sb])
    ident = jnp.stack([jnp.ones_like(cos), jnp.zeros_like(cos), jnp.zeros_like(cos)])
    return jnp.stack([base * SCALE, base, ident])


def kernel(x, c, ctx, c_ctx, w_mod, b_mod, w_in, attn_sink, conv_w, conv_b, conv_norm_g, conv_norm_b,
           w_attn_proj, w_conv_proj, w_out, ln1_g, ln1_b, w_ffn_in, w_ffn_out, ln2_g, ln2_b):
    b, s, d = x.shape
    l = ctx.shape[1]
    m = b * s

    c_all = jnp.concatenate([c, c_ctx[None, :], jnp.zeros((16 - b - 1, d), _F32)], axis=0)
    mod_all = _mod_call(c_all, w_mod[0], b_mod[0][None, :])
    mod = mod_all[:b].reshape(b, 6, d)
    mod_c = mod_all[b:b + 1].reshape(1, 6, d)

    w_in_b = w_in[0].astype(_BF16)
    x2d = x.reshape(m, d)
    h = _ln_mod_call(x2d, mod, s, 0, 1, "ln_mod_x")
    hc = _ln_mod_call(ctx.reshape(b * l, d), mod_c, b * l, 0, 1, "ln_mod_ctx")

    tab = _rope_tables(s)
    qkv = _qkv_call(h, w_in_b, tab, s).reshape(b, s, OFF_GLU)
    kvc = _ctx_kv_call(hc, w_in_b).reshape(b, l, 2 * KV_COLS)
    u = _glu_call(h, w_in_b).reshape(b, s, d)
    gates = _gate_call(h, w_in_b)

    attn = _attn_call(qkv, kvc, attn_sink[0]).reshape(m, d)
    y = _conv_call(u, conv_w[0], conv_b[0][None, :]).reshape(m, d)

    yc = _ln_silu_call(y, conv_norm_g[0][None, :], conv_norm_b[0][None, :])
    mixed = _merge_call(attn, yc, gates, w_attn_proj[0].astype(_BF16), w_conv_proj[0].astype(_BF16))
    x1, h2 = _outproj_call(mixed, x2d, mod, w_out[0].astype(_BF16), ln1_g[0][None, :], ln1_b[0][None, :], s)
    out = _ffn_call(h2, x1, mod, w_ffn_in[0].astype(_BF16), w_ffn_out[0].astype(_BF16),
                    ln2_g[0][None, :], ln2_b[0][None, :], s)
    return out.reshape(b, s, d)
```

```python
import functools

import jax
import jax.numpy as jnp
from jax import lax
from jax.experimental import pallas as pl
from jax.experimental.pallas import tpu as pltpu

D_MODEL = 2048
GRID_W = 64
HEAD_DIM = 128
N_HEADS = 16
N_KV_HEADS = 4
GROUP = N_HEADS // N_KV_HEADS
BLOCK = 128
CONV_K = 31
CONV_HALO = 16
SUB = 8
LANES = 128
ROPE_BASE = 10000.0
DEPTH = 1
ALPHA = (2.0 * DEPTH) ** 0.25
EPS = 1e-6
NEG_INF = -1e30
SCALE = HEAD_DIM ** -0.5

Q_COLS = N_HEADS * HEAD_DIM
KV_COLS = N_KV_HEADS * HEAD_DIM
OFF_K = Q_COLS
OFF_V = OFF_K + KV_COLS
OFF_GLU = OFF_V + KV_COLS
OFF_GATE = OFF_GLU + 2 * D_MODEL

VMEM_LIMIT = 56 * 1024 * 1024
VMEM_LIMIT_MAX = 62 * 1024 * 1024

_BF16 = jnp.bfloat16
_F32 = jnp.float32


def _params(sem, vmem_limit=VMEM_LIMIT):
    return pltpu.CompilerParams(dimension_semantics=sem, vmem_limit_bytes=vmem_limit)


def _ln(xf):
    mu = jnp.mean(xf, axis=-1, keepdims=True)
    xc = xf - mu
    var = jnp.mean(xc * xc, axis=-1, keepdims=True)
    return xc * lax.rsqrt(var + EPS)


def _mod_kernel(c_ref, w_ref, b_ref, o_ref):
    c = c_ref[...]
    s = (c * jax.nn.sigmoid(c)).astype(_BF16)
    o_ref[...] = jnp.dot(s, w_ref[...].astype(_BF16), preferred_element_type=_F32) + b_ref[...]


def _mod_call(c_all, w_mod, b_mod):
    rows, d = c_all.shape
    n = w_mod.shape[1]
    tn = 1024
    return pl.pallas_call(
        _mod_kernel,
        out_shape=jax.ShapeDtypeStruct((rows, n), _F32),
        grid=(n // tn,),
        in_specs=[pl.BlockSpec((rows, d), lambda j: (0, 0)),
                  pl.BlockSpec((d, tn), lambda j: (0, j)),
                  pl.BlockSpec((1, tn), lambda j: (0, j))],
        out_specs=pl.BlockSpec((rows, tn), lambda j: (0, j)),
        compiler_params=_params(("arbitrary",)),
        name="mod",
    )(c_all, w_mod, b_mod)


def _ln_mod_kernel(x_ref, mod_ref, o_ref, *, shift_row, scale_row):
    y = _ln(x_ref[...])
    shift = mod_ref[0, shift_row:shift_row + 1, :]
    scale = mod_ref[0, scale_row:scale_row + 1, :]
    o_ref[...] = (y * (1.0 + scale) + shift).astype(o_ref.dtype)


def _ln_mod_call(x2d, mod, rows_per_group, shift_row, scale_row, name):
    r, d = x2d.shape
    tr = 512
    tiles_per_group = rows_per_group // tr
    return pl.pallas_call(
        functools.partial(_ln_mod_kernel, shift_row=shift_row, scale_row=scale_row),
        out_shape=jax.ShapeDtypeStruct((r, d), _BF16),
        grid=(r // tr,),
        in_specs=[pl.BlockSpec((tr, d), lambda i: (i, 0)),
                  pl.BlockSpec((1, 6, d), lambda i: (i // tiles_per_group, 0, 0))],
        out_specs=pl.BlockSpec((tr, d), lambda i: (i, 0)),
        compiler_params=_params(("parallel",)),
        name=name,
    )(x2d, mod)


def _rope(xs, c, sa, sb):
    return xs * c + pltpu.roll(xs, 96, 1) * sa + pltpu.roll(xs, 32, 1) * sb


def _qkv_kernel(h_ref, w_ref, tab_lo_ref, tab_hi_ref, o_ref):
    acc = jnp.dot(h_ref[...], w_ref[...], preferred_element_type=_F32)
    n_sl = acc.shape[1] // HEAD_DIM
    for hh in range(n_sl):
        tab_ref = tab_lo_ref if hh < n_sl // 2 else tab_hi_ref
        sl = slice(hh * HEAD_DIM, (hh + 1) * HEAD_DIM)
        o_ref[:, sl] = _rope(acc[:, sl], tab_ref[0, 0], tab_ref[0, 1], tab_ref[0, 2]).astype(o_ref.dtype)


def _qkv_call(h, w_in, tab, seq):
    m, d = h.shape
    tm, tn = 1024, 1024
    n_out = OFF_GLU
    t_blocks = seq // tm
    q_tiles = Q_COLS // tn
    tab_blk = (1, 3, tm, HEAD_DIM)
    return pl.pallas_call(
        _qkv_kernel,
        out_shape=jax.ShapeDtypeStruct((m, n_out), _BF16),
        grid=(m // tm, n_out // tn),
        in_specs=[pl.BlockSpec((tm, d), lambda i, j: (i, 0)),
                  pl.BlockSpec((d, tn), lambda i, j: (0, j)),
                  pl.BlockSpec(tab_blk, lambda i, j: (jnp.where(j < q_tiles, 0, 1), 0, i % t_blocks, 0)),
                  pl.BlockSpec(tab_blk, lambda i, j: (jnp.where(j < q_tiles, 0, 2), 0, i % t_blocks, 0))],
        out_specs=pl.BlockSpec((tm, tn), lambda i, j: (i, j)),
        compiler_params=_params(("parallel", "arbitrary")),
        name="proj_qkv",
    )(h, w_in, tab, tab)


def _gate_kernel(h_ref, w_ref, o_ref):
    acc = jnp.dot(h_ref[...], w_ref[...], preferred_element_type=_F32)
    o_ref[...] = jax.nn.sigmoid(acc).astype(o_ref.dtype)


def _gate_call(h, w_in):
    m, d = h.shape
    tm, tn = 1024, 1024
    c0 = OFF_GATE // tn
    return pl.pallas_call(
        _gate_kernel,
        out_shape=jax.ShapeDtypeStruct((m, 2 * D_MODEL), _BF16),
        grid=(m // tm, 2 * D_MODEL // tn),
        in_specs=[pl.BlockSpec((tm, d), lambda i, j: (i, 0)),
                  pl.BlockSpec((d, tn), lambda i, j: (0, c0 + j))],
        out_specs=pl.BlockSpec((tm, tn), lambda i, j: (i, j)),
        compiler_params=_params(("parallel", "arbitrary")),
        name="proj_gate",
    )(h, w_in)


def _matmul_kernel(a_ref, b_ref, o_ref):
    o_ref[...] = jnp.dot(a_ref[...], b_ref[...], preferred_element_type=_F32).astype(o_ref.dtype)


def _ctx_kv_call(hc, w_in):
    m, d = hc.shape
    tm, tn = 1024, 1024
    c0 = OFF_K // tn
    return pl.pallas_call(
        _matmul_kernel,
        out_shape=jax.ShapeDtypeStruct((m, 2 * KV_COLS), _BF16),
        grid=(m // tm,),
        in_specs=[pl.BlockSpec((tm, d), lambda i: (i, 0)),
                  pl.BlockSpec((d, tn), lambda i: (0, c0))],
        out_specs=pl.BlockSpec((tm, tn), lambda i: (i, 0)),
        compiler_params=_params(("parallel",)),
        name="ctx_kv",
    )(hc, w_in)


def _attn_kernel(sink_ref, q_ref, kp_ref, ko_ref, kn_ref, vp_ref, vo_ref, vn_ref, kc_ref, vc_ref, o_ref):
    i = pl.program_id(1)
    nb = pl.num_programs(1)
    rows = GROUP * BLOCK
    row = lax.broadcasted_iota(jnp.int32, (rows, BLOCK), 0)
    ii = jnp.bitwise_and(row, BLOCK - 1)
    jj = lax.broadcasted_iota(jnp.int32, (rows, BLOCK), 1)
    mask_p = jnp.logical_and(jj >= ii, i > 0)
    mask_n = jnp.logical_and(jj <= ii, i < nb - 1)
    row1 = lax.broadcasted_iota(jnp.int32, (rows, 1), 0)
    dn_t = (((1,), (1,)), ((), ()))
    for g in range(N_KV_HEADS):
        ks = slice(g * HEAD_DIM, (g + 1) * HEAD_DIM)
        q4 = jnp.concatenate(
            [q_ref[0, :, (g * GROUP + h) * HEAD_DIM:(g * GROUP + h + 1) * HEAD_DIM] for h in range(GROUP)], axis=0)
        k_all = jnp.concatenate([kp_ref[0, :, ks], ko_ref[0, :, ks], kn_ref[0, :, ks], kc_ref[0, :, ks]], axis=0)
        v_all = jnp.concatenate([vp_ref[0, :, ks], vo_ref[0, :, ks], vn_ref[0, :, ks], vc_ref[0, :, ks]], axis=0)
        s = lax.dot_general(q4, k_all, dn_t, preferred_element_type=_F32)
        s_p = jnp.where(mask_p, s[:, 0:BLOCK], NEG_INF)
        s_o = s[:, BLOCK:2 * BLOCK]
        s_n = jnp.where(mask_n, s[:, 2 * BLOCK:3 * BLOCK], NEG_INF)
        n_ctx = (s.shape[1] - 3 * BLOCK) // BLOCK
        s_c = [s[:, (3 + t) * BLOCK:(4 + t) * BLOCK] for t in range(n_ctx)]
        sink = jnp.full((rows, 1), sink_ref[g * GROUP + GROUP - 1], _F32)
        for h in range(GROUP - 2, -1, -1):
            sink = jnp.where(row1 < (h + 1) * BLOCK, sink_ref[g * GROUP + h], sink)
        mx = jnp.maximum(jnp.maximum(s_p, s_o), s_n)
        for t in s_c:
            mx = jnp.maximum(mx, t)
        m = jnp.maximum(jnp.max(mx, axis=-1, keepdims=True), sink)
        parts = [jnp.exp(t - m) for t in [s_p, s_o, s_n] + s_c]
        tot = parts[0]
        for t in parts[1:]:
            tot = tot + t
        denom = jnp.sum(tot, axis=-1, keepdims=True) + jnp.exp(sink - m)
        p = jnp.concatenate([t.astype(_BF16) for t in parts], axis=1)
        o = jnp.dot(p, v_all, preferred_element_type=_F32) / denom
        for h in range(GROUP):
            c0 = (g * GROUP + h) * HEAD_DIM
            o_ref[0, :, c0:c0 + HEAD_DIM] = o[h * BLOCK:(h + 1) * BLOCK, :].astype(o_ref.dtype)


def _attn_call(qkv, kvc, sink):
    b, s, _ = qkv.shape
    l = kvc.shape[1]
    nb = s // BLOCK
    kcol = OFF_K // KV_COLS
    vcol = OFF_V // KV_COLS
    kv_blk = (1, BLOCK, KV_COLS)
    return pl.pallas_call(
        _attn_kernel,
        out_shape=jax.ShapeDtypeStruct((b, s, Q_COLS), _BF16),
        grid=(b, nb),
        in_specs=[pl.BlockSpec(memory_space=pltpu.SMEM),
                  pl.BlockSpec((1, BLOCK, Q_COLS), lambda bi, i: (bi, i, 0)),
                  pl.BlockSpec(kv_blk, lambda bi, i: (bi, jnp.maximum(i - 1, 0), kcol)),
                  pl.BlockSpec(kv_blk, lambda bi, i: (bi, i, kcol)),
                  pl.BlockSpec(kv_blk, lambda bi, i: (bi, jnp.minimum(i + 1, nb - 1), kcol)),
                  pl.BlockSpec(kv_blk, lambda bi, i: (bi, jnp.maximum(i - 1, 0), vcol)),
                  pl.BlockSpec(kv_blk, lambda bi, i: (bi, i, vcol)),
                  pl.BlockSpec(kv_blk, lambda bi, i: (bi, jnp.minimum(i + 1, nb - 1), vcol)),
                  pl.BlockSpec((1, l, KV_COLS), lambda bi, i: (bi, 0, 0)),
                  pl.BlockSpec((1, l, KV_COLS), lambda bi, i: (bi, 0, 1))],
        out_specs=pl.BlockSpec((1, BLOCK, Q_COLS), lambda bi, i: (bi, i, 0)),
        compiler_params=_params(("parallel", "arbitrary")),
        name="attn",
    )(sink, qkv, qkv, qkv, qkv, qkv, qkv, qkv, kvc, kvc)


def _glu_conv_kernel(h_ref, wa_ref, wg_ref, cw_ref, cb_ref, o_ref, win_even_ref, win_odd_ref, wb_ref, y_ref):
    t = pl.program_id(0)
    s, ct = h_ref.shape[0], wa_ref.shape[1]
    part = s // SUB
    n_lt = ct // LANES
    rc = 8
    first = CONV_HALO - CONV_K // 2

    @pl.when(t == 0)
    def _():
        win_odd_ref[...] = jnp.zeros(win_odd_ref.shape, _F32)

    def project(cur_ref):
        h = h_ref[...]
        a = jnp.dot(h, wa_ref[...], preferred_element_type=_F32)
        g = jnp.dot(h, wg_ref[...], preferred_element_type=_F32)
        u = a * jax.nn.sigmoid(g)
        zeros = jnp.zeros((CONV_HALO, LANES), _F32)
        for lt in range(n_lt):
            ls = slice(lt * LANES, (lt + 1) * LANES)
            for p in range(SUB):
                lo = u[p * part - CONV_HALO:p * part, ls] if p > 0 else zeros
                hi = u[(p + 1) * part:(p + 1) * part + CONV_HALO, ls] if p < SUB - 1 else zeros
                cur_ref[lt, pl.ds(p, CONV_HALO, stride=SUB), :] = lo
                cur_ref[lt, pl.ds(CONV_HALO * SUB + p, part, stride=SUB), :] = u[p * part:(p + 1) * part, ls]
                cur_ref[lt, pl.ds((CONV_HALO + part) * SUB + p, CONV_HALO, stride=SUB), :] = hi

    def convolve(prev_ref):
        for k in range(CONV_K):
            wb_ref[k * SUB:(k + 1) * SUB, :] = jnp.broadcast_to(cw_ref[k:k + 1, :], (SUB, ct))
        bias = jnp.broadcast_to(cb_ref[...], (SUB, ct))
        for c in range(part // rc):
            r0 = c * rc
            for lt in range(n_lt):
                ls = slice(lt * LANES, (lt + 1) * LANES)
                accs = [bias[:, ls] for _ in range(rc)]
                for k in range(CONV_K):
                    wk = wb_ref[k * SUB:(k + 1) * SUB, ls]
                    for j in range(rc):
                        q = (r0 + j + k + first) * SUB
                        accs[j] = accs[j] + prev_ref[lt, q:q + SUB, :] * wk
                for j in range(rc):
                    y_ref[lt, (r0 + j) * SUB:(r0 + j + 1) * SUB, :] = accs[j]
        for p in range(SUB):
            for lt in range(n_lt):
                o_ref[0, p * part:(p + 1) * part, lt * LANES:(lt + 1) * LANES] = y_ref[
                    lt, pl.ds(p, part, stride=SUB), :]

    @pl.when(t % 2 == 0)
    def _():
        project(win_even_ref)
        convolve(win_odd_ref)

    @pl.when(t % 2 == 1)
    def _():
        project(win_odd_ref)
        convolve(win_even_ref)


def _glu_conv_call(h, w_in, conv_w, conv_b, batch, seq):
    m, d = h.shape
    ct = 256
    part = seq // SUB
    n_ct = D_MODEL // ct
    n_tiles = batch * n_ct
    a0 = OFF_GLU // ct
    g0 = (OFF_GLU + D_MODEL) // ct

    def cur(t):
        return jnp.minimum(t, n_tiles - 1)

    def prev(t):
        return jnp.maximum(t - 1, 0)

    win = pltpu.VMEM((ct // LANES, (part + 2 * CONV_HALO) * SUB, LANES), _F32)
    return pl.pallas_call(
        _glu_conv_kernel,
        out_shape=jax.ShapeDtypeStruct((batch, seq, D_MODEL), _F32),
        grid=(n_tiles + 1,),
        in_specs=[pl.BlockSpec((seq, d), lambda t: (cur(t) // n_ct, 0)),
                  pl.BlockSpec((d, ct), lambda t: (0, a0 + cur(t) % n_ct)),
                  pl.BlockSpec((d, ct), lambda t: (0, g0 + cur(t) % n_ct)),
                  pl.BlockSpec((CONV_K, ct), lambda t: (0, prev(t) % n_ct)),
                  pl.BlockSpec((1, ct), lambda t: (0, prev(t) % n_ct))],
        out_specs=pl.BlockSpec((1, seq, ct), lambda t: (prev(t) // n_ct, 0, prev(t) % n_ct)),
        scratch_shapes=[win, win,
                        pltpu.VMEM((CONV_K * SUB, ct), _F32),
                        pltpu.VMEM((ct // LANES, seq, LANES), _F32)],
        compiler_params=_params(("arbitrary",)),
        name="glu_conv",
    )(h, w_in, w_in, conv_w, conv_b)


def _ln_silu_kernel(y_ref, g_ref, b_ref, o_ref):
    z = _ln(y_ref[...]) * g_ref[...] + b_ref[...]
    o_ref[...] = (z * jax.nn.sigmoid(z)).astype(o_ref.dtype)


def _ln_silu_call(y, norm_g, norm_b):
    m, d = y.shape
    tr = 512
    return pl.pallas_call(
        _ln_silu_kernel,
        out_shape=jax.ShapeDtypeStruct((m, d), _BF16),
        grid=(m // tr,),
        in_specs=[pl.BlockSpec((tr, d), lambda i: (i, 0)),
                  pl.BlockSpec((1, d), lambda i: (0, 0)),
                  pl.BlockSpec((1, d), lambda i: (0, 0))],
        out_specs=pl.BlockSpec((tr, d), lambda i: (i, 0)),
        compiler_params=_params(("parallel",)),
        name="ln_silu",
    )(y, norm_g, norm_b)


def _merge_kernel(attn_ref, yc_ref, ga_ref, gc_ref, wa_ref, wc_ref, o_ref):
    ya = jnp.dot(attn_ref[...], wa_ref[...], preferred_element_type=_F32)
    yc = jnp.dot(yc_ref[...], wc_ref[...], preferred_element_type=_F32)
    o_ref[...] = (ga_ref[...].astype(_F32) * ya + gc_ref[...].astype(_F32) * yc).astype(o_ref.dtype)


def _merge_call(attn, yc, gates, w_attn, w_conv):
    m, d = attn.shape
    tm, tn = 1024, 1024
    nj = d // tn
    return pl.pallas_call(
        _merge_kernel,
        out_shape=jax.ShapeDtypeStruct((m, d), _BF16),
        grid=(m // tm, nj),
        in_specs=[pl.BlockSpec((tm, d), lambda i, j: (i, 0)),
                  pl.BlockSpec((tm, d), lambda i, j: (i, 0)),
                  pl.BlockSpec((tm, tn), lambda i, j: (i, j)),
                  pl.BlockSpec((tm, tn), lambda i, j: (i, nj + j)),
                  pl.BlockSpec((d, tn), lambda i, j: (0, j)),
                  pl.BlockSpec((d, tn), lambda i, j: (0, j))],
        out_specs=pl.BlockSpec((tm, tn), lambda i, j: (i, j)),
        compiler_params=_params(("parallel", "arbitrary")),
        name="merge",
    )(attn, yc, gates, gates, w_attn, w_conv)


def _outproj_kernel(m_ref, x_ref, mod_ref, w_ref, g_ref, b_ref, x1_ref, h2_ref):
    mix = jnp.dot(m_ref[...], w_ref[...], preferred_element_type=_F32)
    gate1 = mod_ref[0, 2:3, :]
    shift2 = mod_ref[0, 3:4, :]
    scale2 = mod_ref[0, 4:5, :]
    r = ALPHA * _ln(x_ref[...]) + gate1 * mix
    x1 = _ln(r) * g_ref[...] + b_ref[...]
    x1_ref[...] = x1
    h2_ref[...] = (x1 * (1.0 + scale2) + shift2).astype(h2_ref.dtype)


def _outproj_call(mixed, x2d, mod, w_out, ln_g, ln_b, seq):
    m, d = mixed.shape
    tm = 512
    tiles_per_batch = seq // tm
    return pl.pallas_call(
        _outproj_kernel,
        out_shape=(jax.ShapeDtypeStruct((m, d), _F32), jax.ShapeDtypeStruct((m, d), _BF16)),
        grid=(m // tm,),
        in_specs=[pl.BlockSpec((tm, d), lambda i: (i, 0)),
                  pl.BlockSpec((tm, d), lambda i: (i, 0)),
                  pl.BlockSpec((1, 6, d), lambda i: (i // tiles_per_batch, 0, 0)),
                  pl.BlockSpec((d, d), lambda i: (0, 0)),
                  pl.BlockSpec((1, d), lambda i: (0, 0)),
                  pl.BlockSpec((1, d), lambda i: (0, 0))],
        out_specs=(pl.BlockSpec((tm, d), lambda i: (i, 0)),
                   pl.BlockSpec((tm, d), lambda i: (i, 0))),
        compiler_params=_params(("parallel",)),
        name="outproj",
    )(mixed, x2d, mod, w_out, ln_g, ln_b)


def _ffn_kernel(h_ref, x1_ref, mod_ref, wg_ref, wu_ref, wo_ref, g_ref, b_ref, o_ref, a_ref):
    k = pl.program_id(1)

    @pl.when(k == 0)
    def _():
        o_ref[...] = jnp.zeros_like(o_ref)

    h = h_ref[...]
    nc = 256
    for n in range(wg_ref.shape[1] // nc):
        sl = slice(n * nc, (n + 1) * nc)
        g = jnp.dot(h, wg_ref[:, sl], preferred_element_type=_F32)
        u = jnp.dot(h, wu_ref[:, sl], preferred_element_type=_F32)
        a_ref[:, sl] = (g * jax.nn.sigmoid(g) * u).astype(a_ref.dtype)
    nc = 512
    for n in range(o_ref.shape[1] // nc):
        sl = slice(n * nc, (n + 1) * nc)
        o_ref[:, sl] += jnp.dot(a_ref[...], wo_ref[:, sl], preferred_element_type=_F32)

    @pl.when(k == pl.num_programs(1) - 1)
    def _():
        gate2 = mod_ref[0, 5:6, :]
        r = ALPHA * x1_ref[...] + gate2 * o_ref[...]
        o_ref[...] = _ln(r) * g_ref[...] + b_ref[...]


def _ffn_call(h2, x1, mod, w_ffn_in, w_ffn_out, ln_g, ln_b, seq):
    m, d = h2.shape
    hidden = w_ffn_out.shape[0]
    tm, tk = 1024, 512
    nk = hidden // tk
    tiles_per_batch = seq // tm
    return pl.pallas_call(
        _ffn_kernel,
        out_shape=jax.ShapeDtypeStruct((m, d), _F32),
        grid=(m // tm, nk),
        in_specs=[pl.BlockSpec((tm, d), lambda i, k: (i, 0)),
                  pl.BlockSpec((tm, d), lambda i, k: (i, 0), pipeline_mode=pl.Buffered(1)),
                  pl.BlockSpec((1, 6, d), lambda i, k: (i // tiles_per_batch, 0, 0)),
                  pl.BlockSpec((d, tk), lambda i, k: (0, k)),
                  pl.BlockSpec((d, tk), lambda i, k: (0, nk + k)),
                  pl.BlockSpec((tk, d), lambda i, k: (k, 0)),
                  pl.BlockSpec((1, d), lambda i, k: (0, 0)),
                  pl.BlockSpec((1, d), lambda i, k: (0, 0))],
        out_specs=pl.BlockSpec((tm, d), lambda i, k: (i, 0)),
        scratch_shapes=[pltpu.VMEM((tm, tk), _BF16)],
        compiler_params=_params(("parallel", "arbitrary"), VMEM_LIMIT_MAX),
        name="ffn",
    )(h2, x1, mod, w_ffn_in, w_ffn_in, w_ffn_out, ln_g, ln_b)


def _rope_tables(seq):
    t = jnp.arange(seq, dtype=jnp.int32)
    n_freq = HEAD_DIM // 4
    inv_freq = ROPE_BASE ** (-jnp.arange(n_freq, dtype=_F32) / n_freq)
    ang_r = (t // GRID_W).astype(_F32)[:, None] * inv_freq
    ang_c = (t % GRID_W).astype(_F32)[:, None] * inv_freq
    zero = jnp.zeros_like(ang_r)
    cos = jnp.concatenate([jnp.cos(ang_r), jnp.cos(ang_r), jnp.cos(ang_c), jnp.cos(ang_c)], axis=1)
    sa = jnp.concatenate([-jnp.sin(ang_r), zero, -jnp.sin(ang_c), zero], axis=1)
    sb = jnp.concatenate([zero, jnp.sin(ang_r), zero, jnp.sin(ang_c)], axis=1)
    base = jnp.stack([cos, sa, sb])
    ident = jnp.stack([jnp.ones_like(cos), jnp.zeros_like(cos), jnp.zeros_like(cos)])
    return jnp.stack([base * SCALE, base, ident])


def kernel(x, c, ctx, c_ctx, w_mod, b_mod, w_in, attn_sink, conv_w, conv_b, conv_norm_g, conv_norm_b,
           w_attn_proj, w_conv_proj, w_out, ln1_g, ln1_b, w_ffn_in, w_ffn_out, ln2_g, ln2_b):
    b, s, d = x.shape
    l = ctx.shape[1]
    m = b * s

    c_all = jnp.concatenate([c, c_ctx[None, :], jnp.zeros((16 - b - 1, d), _F32)], axis=0)
    mod_all = _mod_call(c_all, w_mod[0], b_mod[0][None, :])
    mod = mod_all[:b].reshape(b, 6, d)
    mod_c = mod_all[b:b + 1].reshape(1, 6, d)

    w_in_b = w_in[0].astype(_BF16)
    x2d = x.reshape(m, d)
    h = _ln_mod_call(x2d, mod, s, 0, 1, "ln_mod_x")
    hc = _ln_mod_call(ctx.reshape(b * l, d), mod_c, b * l, 0, 1, "ln_mod_ctx")

    tab = _rope_tables(s)
    qkv = _qkv_call(h, w_in_b, tab, s).reshape(b, s, OFF_GLU)
    kvc = _ctx_kv_call(hc, w_in_b).reshape(b, l, 2 * KV_COLS)
    gates = _gate_call(h, w_in_b)

    attn = _attn_call(qkv, kvc, attn_sink[0]).reshape(m, d)
    y = _glu_conv_call(h, w_in_b, conv_w[0], conv_b[0][None, :], b, s).reshape(m, d)

    yc = _ln_silu_call(y, conv_norm_g[0][None, :], conv_norm_b[0][None, :])
    mixed = _merge_call(attn, yc, gates, w_attn_proj[0].astype(_BF16), w_conv_proj[0].astype(_BF16))
    x1, h2 = _outproj_call(mixed, x2d, mod, w_out[0].astype(_BF16), ln1_g[0][None, :], ln1_b[0][None, :], s)
    out = _ffn_call(h2, x1, mod, w_ffn_in[0].astype(_BF16), w_ffn_out[0].astype(_BF16),
                    ln2_g[0][None, :], ln2_b[0][None, :], s)
    return out.reshape(b, s, d)
```

```python
import functools

import jax
import jax.numpy as jnp
from jax import lax
from jax.experimental import pallas as pl
from jax.experimental.pallas import tpu as pltpu

D_MODEL = 2048
GRID_W = 64
HEAD_DIM = 128
N_HEADS = 16
N_KV_HEADS = 4
GROUP = N_HEADS // N_KV_HEADS
BLOCK = 128
CONV_K = 31
CONV_HALO = 16
SUB = 8
LANES = 128
ROPE_BASE = 10000.0
DEPTH = 1
ALPHA = (2.0 * DEPTH) ** 0.25
EPS = 1e-6
NEG_INF = -1e30
SCALE = HEAD_DIM ** -0.5
LOG2E = 1.4426950408889634

Q_COLS = N_HEADS * HEAD_DIM
KV_COLS = N_KV_HEADS * HEAD_DIM
OFF_K = Q_COLS
OFF_V = OFF_K + KV_COLS
OFF_GLU = OFF_V + KV_COLS
OFF_GATE = OFF_GLU + 2 * D_MODEL

VMEM_LIMIT = 56 * 1024 * 1024
VMEM_LIMIT_MAX = 62 * 1024 * 1024

_BF16 = jnp.bfloat16
_F32 = jnp.float32


def _params(sem, vmem_limit=VMEM_LIMIT):
    return pltpu.CompilerParams(dimension_semantics=sem, vmem_limit_bytes=vmem_limit)


def _ln(xf):
    mu = jnp.mean(xf, axis=-1, keepdims=True)
    xc = xf - mu
    var = jnp.mean(xc * xc, axis=-1, keepdims=True)
    return xc * lax.rsqrt(var + EPS)


def _mod_kernel(c_ref, w_ref, b_ref, o_ref):
    c = c_ref[...]
    s = (c * jax.nn.sigmoid(c)).astype(_BF16)
    o_ref[...] = jnp.dot(s, w_ref[...].astype(_BF16), preferred_element_type=_F32) + b_ref[...]


def _mod_call(c_all, w_mod, b_mod):
    rows, d = c_all.shape
    n = w_mod.shape[1]
    tn = 1024
    return pl.pallas_call(
        _mod_kernel,
        out_shape=jax.ShapeDtypeStruct((rows, n), _F32),
        grid=(n // tn,),
        in_specs=[pl.BlockSpec((rows, d), lambda j: (0, 0)),
                  pl.BlockSpec((d, tn), lambda j: (0, j)),
                  pl.BlockSpec((1, tn), lambda j: (0, j))],
        out_specs=pl.BlockSpec((rows, tn), lambda j: (0, j)),
        compiler_params=_params(("arbitrary",)),
        name="mod",
    )(c_all, w_mod, b_mod)


def _ln_mod_kernel(x_ref, mod_ref, o_ref, *, shift_row, scale_row):
    y = _ln(x_ref[...])
    shift = mod_ref[0, shift_row:shift_row + 1, :]
    scale = mod_ref[0, scale_row:scale_row + 1, :]
    o_ref[...] = (y * (1.0 + scale) + shift).astype(o_ref.dtype)


def _ln_mod_call(x2d, mod, rows_per_group, shift_row, scale_row, name):
    r, d = x2d.shape
    tr = 512
    tiles_per_group = rows_per_group // tr
    return pl.pallas_call(
        functools.partial(_ln_mod_kernel, shift_row=shift_row, scale_row=scale_row),
        out_shape=jax.ShapeDtypeStruct((r, d), _BF16),
        grid=(r // tr,),
        in_specs=[pl.BlockSpec((tr, d), lambda i: (i, 0)),
                  pl.BlockSpec((1, 6, d), lambda i: (i // tiles_per_group, 0, 0))],
        out_specs=pl.BlockSpec((tr, d), lambda i: (i, 0)),
        compiler_params=_params(("parallel",)),
        name=name,
    )(x2d, mod)


def _rope(xs, c, sa, sb):
    return xs * c + pltpu.roll(xs, 96, 1) * sa + pltpu.roll(xs, 32, 1) * sb


def _qkv_kernel(h_ref, w_ref, tab_lo_ref, tab_hi_ref, o_ref):
    acc = jnp.dot(h_ref[...], w_ref[...].astype(_BF16), preferred_element_type=_F32)
    n_sl = acc.shape[1] // HEAD_DIM
    for hh in range(n_sl):
        tab_ref = tab_lo_ref if hh < n_sl // 2 else tab_hi_ref
        sl = slice(hh * HEAD_DIM, (hh + 1) * HEAD_DIM)
        o_ref[:, sl] = _rope(acc[:, sl], tab_ref[0, 0], tab_ref[0, 1], tab_ref[0, 2]).astype(o_ref.dtype)


def _qkv_call(h, w_in, tab, seq):
    m, d = h.shape
    tm, tn = 1024, 1024
    n_out = OFF_GLU
    t_blocks = seq // tm
    q_tiles = Q_COLS // tn
    tab_blk = (1, 3, tm, HEAD_DIM)
    return pl.pallas_call(
        _qkv_kernel,
        out_shape=jax.ShapeDtypeStruct((m, n_out), _BF16),
        grid=(m // tm, n_out // tn),
        in_specs=[pl.BlockSpec((tm, d), lambda i, j: (i, 0)),
                  pl.BlockSpec((d, tn), lambda i, j: (0, j)),
                  pl.BlockSpec(tab_blk, lambda i, j: (jnp.where(j < q_tiles, 0, 1), 0, i % t_blocks, 0)),
                  pl.BlockSpec(tab_blk, lambda i, j: (jnp.where(j < q_tiles, 0, 2), 0, i % t_blocks, 0))],
        out_specs=pl.BlockSpec((tm, tn), lambda i, j: (i, j)),
        compiler_params=_params(("parallel", "arbitrary")),
        name="proj_qkv",
    )(h, w_in, tab, tab)


def _gate_kernel(h_ref, w_ref, o_ref):
    acc = jnp.dot(h_ref[...], w_ref[...].astype(_BF16), preferred_element_type=_F32)
    o_ref[...] = jax.nn.sigmoid(acc).astype(o_ref.dtype)


def _gate_call(h, w_in):
    m, d = h.shape
    tm, tn = 1024, 1024
    c0 = OFF_GATE // tn
    return pl.pallas_call(
        _gate_kernel,
        out_shape=jax.ShapeDtypeStruct((m, 2 * D_MODEL), _BF16),
        grid=(m // tm, 2 * D_MODEL // tn),
        in_specs=[pl.BlockSpec((tm, d), lambda i, j: (i, 0)),
                  pl.BlockSpec((d, tn), lambda i, j: (0, c0 + j))],
        out_specs=pl.BlockSpec((tm, tn), lambda i, j: (i, j)),
        compiler_params=_params(("parallel", "arbitrary")),
        name="proj_gate",
    )(h, w_in)


def _matmul_kernel(a_ref, b_ref, o_ref):
    o_ref[...] = jnp.dot(a_ref[...], b_ref[...].astype(_BF16), preferred_element_type=_F32).astype(o_ref.dtype)


def _ctx_kv_call(hc, w_in):
    m, d = hc.shape
    tm, tn = 1024, 1024
    c0 = OFF_K // tn
    return pl.pallas_call(
        _matmul_kernel,
        out_shape=jax.ShapeDtypeStruct((m, 2 * KV_COLS), _BF16),
        grid=(m // tm,),
        in_specs=[pl.BlockSpec((tm, d), lambda i: (i, 0)),
                  pl.BlockSpec((d, tn), lambda i: (0, c0))],
        out_specs=pl.BlockSpec((tm, tn), lambda i: (i, 0)),
        compiler_params=_params(("parallel",)),
        name="ctx_kv",
    )(hc, w_in)


def _attn_kernel(sink_ref, q_ref, kp_ref, ko_ref, kn_ref, vp_ref, vo_ref, vn_ref, kc_ref, vc_ref, o_ref):
    i = pl.program_id(1)
    nb = pl.num_programs(1)
    rows = GROUP * BLOCK
    row = lax.broadcasted_iota(jnp.int32, (rows, BLOCK), 0)
    ii = jnp.bitwise_and(row, BLOCK - 1)
    jj = lax.broadcasted_iota(jnp.int32, (rows, BLOCK), 1)
    mask_p = jnp.logical_and(jj >= ii, i > 0)
    mask_n = jnp.logical_and(jj <= ii, i < nb - 1)
    row1 = lax.broadcasted_iota(jnp.int32, (rows, 1), 0)
    dn_t = (((1,), (1,)), ((), ()))
    for g in range(N_KV_HEADS):
        ks = slice(g * HEAD_DIM, (g + 1) * HEAD_DIM)
        q4 = jnp.concatenate(
            [q_ref[0, :, (g * GROUP + h) * HEAD_DIM:(g * GROUP + h + 1) * HEAD_DIM] for h in range(GROUP)], axis=0)
        k_all = jnp.concatenate([kp_ref[0, :, ks], ko_ref[0, :, ks], kn_ref[0, :, ks], kc_ref[0, :, ks]], axis=0)
        v_all = jnp.concatenate([vp_ref[0, :, ks], vo_ref[0, :, ks], vn_ref[0, :, ks], vc_ref[0, :, ks]], axis=0)
        s = lax.dot_general(q4, k_all, dn_t, preferred_element_type=_F32)
        s_p = jnp.where(mask_p, s[:, 0:BLOCK], NEG_INF)
        s_o = s[:, BLOCK:2 * BLOCK]
        s_n = jnp.where(mask_n, s[:, 2 * BLOCK:3 * BLOCK], NEG_INF)
        n_ctx = (s.shape[1] - 3 * BLOCK) // BLOCK
        s_c = [s[:, (3 + t) * BLOCK:(4 + t) * BLOCK] for t in range(n_ctx)]
        sink = jnp.full((rows, 1), sink_ref[g * GROUP + GROUP - 1] * LOG2E, _F32)
        for h in range(GROUP - 2, -1, -1):
            sink = jnp.where(row1 < (h + 1) * BLOCK, sink_ref[g * GROUP + h] * LOG2E, sink)
        mx = jnp.maximum(jnp.maximum(s_p, s_o), s_n)
        for t in s_c:
            mx = jnp.maximum(mx, t)
        m = jnp.maximum(jnp.max(mx, axis=-1, keepdims=True), sink)
        parts = [jnp.exp2(t - m) for t in [s_p, s_o, s_n] + s_c]
        tot = parts[0]
        for t in parts[1:]:
            tot = tot + t
        denom = jnp.sum(tot, axis=-1, keepdims=True) + jnp.exp2(sink - m)
        p = jnp.concatenate([t.astype(_BF16) for t in parts], axis=1)
        o = jnp.dot(p, v_all, preferred_element_type=_F32) / denom
        for h in range(GROUP):
            c0 = (g * GROUP + h) * HEAD_DIM
            o_ref[0, :, c0:c0 + HEAD_DIM] = o[h * BLOCK:(h + 1) * BLOCK, :].astype(o_ref.dtype)


def _attn_call(qkv, kvc, sink):
    b, s, _ = qkv.shape
    l = kvc.shape[1]
    nb = s // BLOCK
    kcol = OFF_K // KV_COLS
    vcol = OFF_V // KV_COLS
    kv_blk = (1, BLOCK, KV_COLS)
    return pl.pallas_call(
        _attn_kernel,
        out_shape=jax.ShapeDtypeStruct((b, s, Q_COLS), _BF16),
        grid=(b, nb),
        in_specs=[pl.BlockSpec(memory_space=pltpu.SMEM),
                  pl.BlockSpec((1, BLOCK, Q_COLS), lambda bi, i: (bi, i, 0)),
                  pl.BlockSpec(kv_blk, lambda bi, i: (bi, jnp.maximum(i - 1, 0), kcol)),
                  pl.BlockSpec(kv_blk, lambda bi, i: (bi, i, kcol)),
                  pl.BlockSpec(kv_blk, lambda bi, i: (bi, jnp.minimum(i + 1, nb - 1), kcol)),
                  pl.BlockSpec(kv_blk, lambda bi, i: (bi, jnp.maximum(i - 1, 0), vcol)),
                  pl.BlockSpec(kv_blk, lambda bi, i: (bi, i, vcol)),
                  pl.BlockSpec(kv_blk, lambda bi, i: (bi, jnp.minimum(i + 1, nb - 1), vcol)),
                  pl.BlockSpec((1, l, KV_COLS), lambda bi, i: (bi, 0, 0)),
                  pl.BlockSpec((1, l, KV_COLS), lambda bi, i: (bi, 0, 1))],
        out_specs=pl.BlockSpec((1, BLOCK, Q_COLS), lambda bi, i: (bi, i, 0)),
        compiler_params=_params(("parallel", "arbitrary")),
        name="attn",
    )(sink, qkv, qkv, qkv, qkv, qkv, qkv, qkv, kvc, kvc)


def _glu_conv_kernel(h_ref, wa_ref, wg_ref, cw_ref, cb_ref, o_ref, win_even_ref, win_odd_ref, wb_ref, y_ref):
    t = pl.program_id(0)
    s, ct = h_ref.shape[0], wa_ref.shape[1]
    part = s // SUB
    n_lt = ct // LANES
    rc = 8
    first = CONV_HALO - CONV_K // 2

    @pl.when(t == 0)
    def _():
        win_odd_ref[...] = jnp.zeros(win_odd_ref.shape, _F32)

    def project(cur_ref):
        h = h_ref[...]
        a = jnp.dot(h, wa_ref[...].astype(_BF16), preferred_element_type=_F32)
        g = jnp.dot(h, wg_ref[...].astype(_BF16), preferred_element_type=_F32)
        u = a * jax.nn.sigmoid(g)
        zeros = jnp.zeros((CONV_HALO, LANES), _F32)
        for lt in range(n_lt):
            ls = slice(lt * LANES, (lt + 1) * LANES)
            for p in range(SUB):
                lo = u[p * part - CONV_HALO:p * part, ls] if p > 0 else zeros
                hi = u[(p + 1) * part:(p + 1) * part + CONV_HALO, ls] if p < SUB - 1 else zeros
                cur_ref[lt, pl.ds(p, CONV_HALO, stride=SUB), :] = lo
                cur_ref[lt, pl.ds(CONV_HALO * SUB + p, part, stride=SUB), :] = u[p * part:(p + 1) * part, ls]
                cur_ref[lt, pl.ds((CONV_HALO + part) * SUB + p, CONV_HALO, stride=SUB), :] = hi

    def convolve(prev_ref):
        for k in range(CONV_K):
            wb_ref[k * SUB:(k + 1) * SUB, :] = jnp.broadcast_to(cw_ref[k:k + 1, :], (SUB, ct))
        bias = jnp.broadcast_to(cb_ref[...], (SUB, ct))
        for c in range(part // rc):
            r0 = c * rc
            for lt in range(n_lt):
                ls = slice(lt * LANES, (lt + 1) * LANES)
                accs = [bias[:, ls] for _ in range(rc)]
                for k in range(CONV_K):
                    wk = wb_ref[k * SUB:(k + 1) * SUB, ls]
                    for j in range(rc):
                        q = (r0 + j + k + first) * SUB
                        accs[j] = accs[j] + prev_ref[lt, q:q + SUB, :] * wk
                for j in range(rc):
                    y_ref[lt, (r0 + j) * SUB:(r0 + j + 1) * SUB, :] = accs[j]
        for p in range(SUB):
            for lt in range(n_lt):
                o_ref[0, p * part:(p + 1) * part, lt * LANES:(lt + 1) * LANES] = y_ref[
                    lt, pl.ds(p, part, stride=SUB), :].astype(o_ref.dtype)

    @pl.when(t % 2 == 0)
    def _():
        project(win_even_ref)
        convolve(win_odd_ref)

    @pl.when(t % 2 == 1)
    def _():
        project(win_odd_ref)
        convolve(win_even_ref)


def _glu_conv_call(h, w_in, conv_w, conv_b, batch, seq):
    m, d = h.shape
    ct = 256
    part = seq // SUB
    n_ct = D_MODEL // ct
    n_tiles = batch * n_ct
    a0 = OFF_GLU // ct
    g0 = (OFF_GLU + D_MODEL) // ct

    def cur(t):
        return jnp.minimum(t, n_tiles - 1)

    def prev(t):
        return jnp.maximum(t - 1, 0)

    win = pltpu.VMEM((ct // LANES, (part + 2 * CONV_HALO) * SUB, LANES), _F32)
    return pl.pallas_call(
        _glu_conv_kernel,
        out_shape=jax.ShapeDtypeStruct((batch, seq, D_MODEL), _BF16),
        grid=(n_tiles + 1,),
        in_specs=[pl.BlockSpec((seq, d), lambda t: (cur(t) // n_ct, 0)),
                  pl.BlockSpec((d, ct), lambda t: (0, a0 + cur(t) % n_ct)),
                  pl.BlockSpec((d, ct), lambda t: (0, g0 + cur(t) % n_ct)),
                  pl.BlockSpec((CONV_K, ct), lambda t: (0, prev(t) % n_ct)),
                  pl.BlockSpec((1, ct), lambda t: (0, prev(t) % n_ct))],
        out_specs=pl.BlockSpec((1, seq, ct), lambda t: (prev(t) // n_ct, 0, prev(t) % n_ct)),
        scratch_shapes=[win, win,
                        pltpu.VMEM((CONV_K * SUB, ct), _F32),
                        pltpu.VMEM((ct // LANES, seq, LANES), _F32)],
        compiler_params=_params(("arbitrary",)),
        name="glu_conv",
    )(h, w_in, w_in, conv_w, conv_b)


def _ln_silu_kernel(y_ref, g_ref, b_ref, o_ref):
    z = _ln(y_ref[...].astype(_F32)) * g_ref[...] + b_ref[...]
    o_ref[...] = (z * jax.nn.sigmoid(z)).astype(o_ref.dtype)


def _ln_silu_call(y, norm_g, norm_b):
    m, d = y.shape
    tr = 512
    return pl.pallas_call(
        _ln_silu_kernel,
        out_shape=jax.ShapeDtypeStruct((m, d), _BF16),
        grid=(m // tr,),
        in_specs=[pl.BlockSpec((tr, d), lambda i: (i, 0)),
                  pl.BlockSpec((1, d), lambda i: (0, 0)),
                  pl.BlockSpec((1, d), lambda i: (0, 0))],
        out_specs=pl.BlockSpec((tr, d), lambda i: (i, 0)),
        compiler_params=_params(("parallel",)),
        name="ln_silu",
    )(y, norm_g, norm_b)


def _merge_kernel(attn_ref, yc_ref, ga_ref, gc_ref, wa_ref, wc_ref, o_ref):
    ya = jnp.dot(attn_ref[...], wa_ref[...], preferred_element_type=_F32)
    yc = jnp.dot(yc_ref[...], wc_ref[...], preferred_element_type=_F32)
    o_ref[...] = (ga_ref[...].astype(_F32) * ya + gc_ref[...].astype(_F32) * yc).astype(o_ref.dtype)


def _merge_call(attn, yc, gates, w_attn, w_conv):
    m, d = attn.shape
    tm, tn = 1024, 1024
    nj = d // tn
    return pl.pallas_call(
        _merge_kernel,
        out_shape=jax.ShapeDtypeStruct((m, d), _BF16),
        grid=(m // tm, nj),
        in_specs=[pl.BlockSpec((tm, d), lambda i, j: (i, 0)),
                  pl.BlockSpec((tm, d), lambda i, j: (i, 0)),
                  pl.BlockSpec((tm, tn), lambda i, j: (i, j)),
                  pl.BlockSpec((tm, tn), lambda i, j: (i, nj + j)),
                  pl.BlockSpec((d, tn), lambda i, j: (0, j)),
                  pl.BlockSpec((d, tn), lambda i, j: (0, j))],
        out_specs=pl.BlockSpec((tm, tn), lambda i, j: (i, j)),
        compiler_params=_params(("parallel", "arbitrary")),
        name="merge",
    )(attn, yc, gates, gates, w_attn, w_conv)


def _outproj_kernel(m_ref, x_ref, mod_ref, w_ref, g_ref, b_ref, x1_ref, h2_ref):
    mix = jnp.dot(m_ref[...], w_ref[...], preferred_element_type=_F32)
    gate1 = mod_ref[0, 2:3, :]
    shift2 = mod_ref[0, 3:4, :]
    scale2 = mod_ref[0, 4:5, :]
    r = ALPHA * _ln(x_ref[...]) + gate1 * mix
    x1 = _ln(r) * g_ref[...] + b_ref[...]
    x1_ref[...] = x1
    h2_ref[...] = (x1 * (1.0 + scale2) + shift2).astype(h2_ref.dtype)


def _outproj_call(mixed, x2d, mod, w_out, ln_g, ln_b, seq):
    m, d = mixed.shape
    tm = 512
    tiles_per_batch = seq // tm
    return pl.pallas_call(
        _outproj_kernel,
        out_shape=(jax.ShapeDtypeStruct((m, d), _F32), jax.ShapeDtypeStruct((m, d), _BF16)),
        grid=(m // tm,),
        in_specs=[pl.BlockSpec((tm, d), lambda i: (i, 0)),
                  pl.BlockSpec((tm, d), lambda i: (i, 0)),
                  pl.BlockSpec((1, 6, d), lambda i: (i // tiles_per_batch, 0, 0)),
                  pl.BlockSpec((d, d), lambda i: (0, 0)),
                  pl.BlockSpec((1, d), lambda i: (0, 0)),
                  pl.BlockSpec((1, d), lambda i: (0, 0))],
        out_specs=(pl.BlockSpec((tm, d), lambda i: (i, 0)),
                   pl.BlockSpec((tm, d), lambda i: (i, 0))),
        compiler_params=_params(("parallel",)),
        name="outproj",
    )(mixed, x2d, mod, w_out, ln_g, ln_b)


def _ffn_kernel(h_ref, x1_hbm_ref, mod_ref, wg_ref, wu_ref, wo_ref, g_ref, b_ref, o_ref, a_ref, x1_ref, x1_sem):
    i = pl.program_id(0)
    k = pl.program_id(1)
    tm = o_ref.shape[0]
    x1_copy = pltpu.make_async_copy(x1_hbm_ref.at[pl.ds(pl.multiple_of(i * tm, tm), tm)], x1_ref, x1_sem)

    @pl.when(k == 0)
    def _():
        x1_copy.start()
        o_ref[...] = jnp.zeros_like(o_ref)

    h = h_ref[...]
    nc = 256
    for n in range(wg_ref.shape[1] // nc):
        sl = slice(n * nc, (n + 1) * nc)
        g = jnp.dot(h, wg_ref[:, sl], preferred_element_type=_F32)
        u = jnp.dot(h, wu_ref[:, sl], preferred_element_type=_F32)
        a_ref[:, sl] = (g * jax.nn.sigmoid(g) * u).astype(a_ref.dtype)
    nc = 512
    for n in range(o_ref.shape[1] // nc):
        sl = slice(n * nc, (n + 1) * nc)
        o_ref[:, sl] += jnp.dot(a_ref[...], wo_ref[:, sl], preferred_element_type=_F32)

    @pl.when(k == pl.num_programs(1) - 1)
    def _():
        x1_copy.wait()
        gate2 = mod_ref[0, 5:6, :]
        r = ALPHA * x1_ref[...] + gate2 * o_ref[...]
        o_ref[...] = _ln(r) * g_ref[...] + b_ref[...]


def _ffn_call(h2, x1, mod, w_ffn_in, w_ffn_out, ln_g, ln_b, seq):
    m, d = h2.shape
    hidden = w_ffn_out.shape[0]
    tm, tk = 1024, 512
    nk = hidden // tk
    tiles_per_batch = seq // tm
    return pl.pallas_call(
        _ffn_kernel,
        out_shape=jax.ShapeDtypeStruct((m, d), _F32),
        grid=(m // tm, nk),
        in_specs=[pl.BlockSpec((tm, d), lambda i, k: (i, 0)),
                  pl.BlockSpec(memory_space=pl.ANY),
                  pl.BlockSpec((1, 6, d), lambda i, k: (i // tiles_per_batch, 0, 0)),
                  pl.BlockSpec((d, tk), lambda i, k: (0, k)),
                  pl.BlockSpec((d, tk), lambda i, k: (0, nk + k)),
                  pl.BlockSpec((tk, d), lambda i, k: (k, 0)),
                  pl.BlockSpec((1, d), lambda i, k: (0, 0)),
                  pl.BlockSpec((1, d), lambda i, k: (0, 0))],
        out_specs=pl.BlockSpec((tm, d), lambda i, k: (i, 0)),
        scratch_shapes=[pltpu.VMEM((tm, tk), _BF16),
                        pltpu.VMEM((tm, d), _F32),
                        pltpu.SemaphoreType.DMA(())],
        compiler_params=_params(("arbitrary", "arbitrary"), VMEM_LIMIT_MAX),
        name="ffn",
    )(h2, x1, mod, w_ffn_in, w_ffn_in, w_ffn_out, ln_g, ln_b)


def _rope_tables(seq):
    t = jnp.arange(seq, dtype=jnp.int32)
    n_freq = HEAD_DIM // 4
    inv_freq = ROPE_BASE ** (-jnp.arange(n_freq, dtype=_F32) / n_freq)
    ang_r = (t // GRID_W).astype(_F32)[:, None] * inv_freq
    ang_c = (t % GRID_W).astype(_F32)[:, None] * inv_freq
    zero = jnp.zeros_like(ang_r)
    cos = jnp.concatenate([jnp.cos(ang_r), jnp.cos(ang_r), jnp.cos(ang_c), jnp.cos(ang_c)], axis=1)
    sa = jnp.concatenate([-jnp.sin(ang_r), zero, -jnp.sin(ang_c), zero], axis=1)
    sb = jnp.concatenate([zero, jnp.sin(ang_r), zero, jnp.sin(ang_c)], axis=1)
    base = jnp.stack([cos, sa, sb])
    ident = jnp.stack([jnp.ones_like(cos), jnp.zeros_like(cos), jnp.zeros_like(cos)])
    return jnp.stack([base * (SCALE * LOG2E), base, ident])


def kernel(x, c, ctx, c_ctx, w_mod, b_mod, w_in, attn_sink, conv_w, conv_b, conv_norm_g, conv_norm_b,
           w_attn_proj, w_conv_proj, w_out, ln1_g, ln1_b, w_ffn_in, w_ffn_out, ln2_g, ln2_b):
    b, s, d = x.shape
    l = ctx.shape[1]
    m = b * s

    c_all = jnp.concatenate([c, c_ctx[None, :], jnp.zeros((16 - b - 1, d), _F32)], axis=0)
    mod_all = _mod_call(c_all, w_mod[0], b_mod[0][None, :])
    mod = mod_all[:b].reshape(b, 6, d)
    mod_c = mod_all[b:b + 1].reshape(1, 6, d)

    w_in0 = w_in[0]
    x2d = x.reshape(m, d)
    h = _ln_mod_call(x2d, mod, s, 0, 1, "ln_mod_x")
    hc = _ln_mod_call(ctx.reshape(b * l, d), mod_c, b * l, 0, 1, "ln_mod_ctx")

    tab = _rope_tables(s)
    qkv = _qkv_call(h, w_in0, tab, s).reshape(b, s, OFF_GLU)
    kvc = _ctx_kv_call(hc, w_in0).reshape(b, l, 2 * KV_COLS)
    gates = _gate_call(h, w_in0)

    attn = _attn_call(qkv, kvc, attn_sink[0]).reshape(m, d)
    y = _glu_conv_call(h, w_in0, conv_w[0], conv_b[0][None, :], b, s).reshape(m, d)

    yc = _ln_silu_call(y, conv_norm_g[0][None, :], conv_norm_b[0][None, :])
    mixed = _merge_call(attn, yc, gates, w_attn_proj[0].astype(_BF16), w_conv_proj[0].astype(_BF16))
    x1, h2 = _outproj_call(mixed, x2d, mod, w_out[0].astype(_BF16), ln1_g[0][None, :], ln1_b[0][None, :], s)
    out = _ffn_call(h2, x1, mod, w_ffn_in[0].astype(_BF16), w_ffn_out[0].astype(_BF16),
                    ln2_g[0][None, :], ln2_b[0][None, :], s)
    return out.reshape(b, s, d)
```

```python
import functools

import jax
import jax.numpy as jnp
from jax import lax
from jax.experimental import pallas as pl
from jax.experimental.pallas import tpu as pltpu

D_MODEL = 2048
GRID_W = 64
HEAD_DIM = 128
N_HEADS = 16
N_KV_HEADS = 4
GROUP = N_HEADS // N_KV_HEADS
BLOCK = 128
CONV_K = 31
CONV_HALO = 16
SUB = 8
LANES = 128
ROPE_BASE = 10000.0
DEPTH = 1
ALPHA = (2.0 * DEPTH) ** 0.25
EPS = 1e-6
NEG_INF = -1e30
SCALE = HEAD_DIM ** -0.5
LOG2E = 1.4426950408889634

Q_COLS = N_HEADS * HEAD_DIM
KV_COLS = N_KV_HEADS * HEAD_DIM
OFF_K = Q_COLS
OFF_V = OFF_K + KV_COLS
OFF_GLU = OFF_V + KV_COLS
OFF_GATE = OFF_GLU + 2 * D_MODEL

VMEM_LIMIT = 56 * 1024 * 1024
VMEM_LIMIT_MAX = 62 * 1024 * 1024

_BF16 = jnp.bfloat16
_F32 = jnp.float32


def _params(sem, vmem_limit=VMEM_LIMIT):
    return pltpu.CompilerParams(dimension_semantics=sem, vmem_limit_bytes=vmem_limit)


def _ln(xf):
    mu = jnp.mean(xf, axis=-1, keepdims=True)
    xc = xf - mu
    var = jnp.mean(xc * xc, axis=-1, keepdims=True)
    return xc * lax.rsqrt(var + EPS)


def _mod_kernel(c_ref, w_ref, b_ref, o_ref):
    c = c_ref[...]
    s = (c * jax.nn.sigmoid(c)).astype(_BF16)
    o_ref[...] = jnp.dot(s, w_ref[...].astype(_BF16), preferred_element_type=_F32) + b_ref[...]


def _mod_call(c_all, w_mod, b_mod):
    rows, d = c_all.shape
    n = w_mod.shape[1]
    tn = 1024
    return pl.pallas_call(
        _mod_kernel,
        out_shape=jax.ShapeDtypeStruct((rows, n), _F32),
        grid=(n // tn,),
        in_specs=[pl.BlockSpec((rows, d), lambda j: (0, 0)),
                  pl.BlockSpec((d, tn), lambda j: (0, j)),
                  pl.BlockSpec((1, tn), lambda j: (0, j))],
        out_specs=pl.BlockSpec((rows, tn), lambda j: (0, j)),
        compiler_params=_params(("arbitrary",)),
        name="mod",
    )(c_all, w_mod, b_mod)


def _ln_mod_kernel(x_ref, mod_ref, o_ref, *, shift_row, scale_row):
    y = _ln(x_ref[...])
    shift = mod_ref[0, shift_row:shift_row + 1, :]
    scale = mod_ref[0, scale_row:scale_row + 1, :]
    o_ref[...] = (y * (1.0 + scale) + shift).astype(o_ref.dtype)


def _ln_mod_call(x2d, mod, rows_per_group, shift_row, scale_row, name):
    r, d = x2d.shape
    tr = 512
    tiles_per_group = rows_per_group // tr
    return pl.pallas_call(
        functools.partial(_ln_mod_kernel, shift_row=shift_row, scale_row=scale_row),
        out_shape=jax.ShapeDtypeStruct((r, d), _BF16),
        grid=(r // tr,),
        in_specs=[pl.BlockSpec((tr, d), lambda i: (i, 0)),
                  pl.BlockSpec((1, 6, d), lambda i: (i // tiles_per_group, 0, 0))],
        out_specs=pl.BlockSpec((tr, d), lambda i: (i, 0)),
        compiler_params=_params(("parallel",)),
        name=name,
    )(x2d, mod)


def _rope(xs, c, sa, sb):
    return xs * c + pltpu.roll(xs, 96, 1) * sa + pltpu.roll(xs, 32, 1) * sb


def _qkv_kernel(h_ref, w_ref, tab_lo_ref, tab_hi_ref, rest_ref, o_ref, rest_o_ref, wb_ref):
    @pl.when(pl.program_id(1) == 0)
    def _():
        wb_ref[...] = w_ref[...].astype(wb_ref.dtype)

    rest_o_ref[...] = rest_ref[...].astype(rest_o_ref.dtype)

    acc = jnp.dot(h_ref[...], wb_ref[...], preferred_element_type=_F32)
    n_sl = acc.shape[1] // HEAD_DIM
    for hh in range(n_sl):
        tab_ref = tab_lo_ref if hh < n_sl // 2 else tab_hi_ref
        sl = slice(hh * HEAD_DIM, (hh + 1) * HEAD_DIM)
        o_ref[:, sl] = _rope(acc[:, sl], tab_ref[0, 0], tab_ref[0, 1], tab_ref[0, 2]).astype(o_ref.dtype)


def _qkv_call(h, w_in, tab, seq):
    m, d = h.shape
    tm, tn = 1024, 1024
    n_out = OFF_GLU
    n_i = m // tm
    t_blocks = seq // tm
    q_tiles = Q_COLS // tn
    tab_blk = (1, 3, tm, HEAD_DIM)
    rest_cols = w_in.shape[1] - OFF_GLU
    rb, cb = 512, 1024
    n_rb = d // rb
    n_blocks = n_rb * (rest_cols // cb)

    def rest_block(j, i):
        return jnp.minimum(j * n_i + i, n_blocks - 1)

    return pl.pallas_call(
        _qkv_kernel,
        out_shape=(jax.ShapeDtypeStruct((m, n_out), _BF16), jax.ShapeDtypeStruct((d, rest_cols), _BF16)),
        grid=(n_out // tn, n_i),
        in_specs=[pl.BlockSpec((tm, d), lambda j, i: (i, 0)),
                  pl.BlockSpec((d, tn), lambda j, i: (0, j)),
                  pl.BlockSpec(tab_blk, lambda j, i: (jnp.where(j < q_tiles, 0, 1), 0, i % t_blocks, 0)),
                  pl.BlockSpec(tab_blk, lambda j, i: (jnp.where(j < q_tiles, 0, 2), 0, i % t_blocks, 0)),
                  pl.BlockSpec((rb, cb), lambda j, i: (rest_block(j, i) % n_rb,
                                                       OFF_GLU // cb + rest_block(j, i) // n_rb))],
        out_specs=(pl.BlockSpec((tm, tn), lambda j, i: (i, j)),
                   pl.BlockSpec((rb, cb), lambda j, i: (rest_block(j, i) % n_rb, rest_block(j, i) // n_rb))),
        scratch_shapes=[pltpu.VMEM((d, tn), _BF16)],
        compiler_params=_params(("arbitrary", "arbitrary")),
        name="proj_qkv",
    )(h, w_in, tab, tab, w_in)


def _gate_kernel(h_ref, w_ref, cast_ref, o_ref, cast_o_ref):
    cast_o_ref[...] = cast_ref[...].astype(cast_o_ref.dtype)
    acc = jnp.dot(h_ref[...], w_ref[...], preferred_element_type=_F32)
    o_ref[...] = jax.nn.sigmoid(acc).astype(o_ref.dtype)


def _gate_call(h, w_rest, w_cast):
    m, d = h.shape
    tm, tn = 1024, 1024
    c0 = (OFF_GATE - OFF_GLU) // tn
    n_j = 2 * D_MODEL // tn
    rows, cols = w_cast.shape
    rb = rows // ((m // tm) * n_j)
    return pl.pallas_call(
        _gate_kernel,
        out_shape=(jax.ShapeDtypeStruct((m, 2 * D_MODEL), _BF16), jax.ShapeDtypeStruct((rows, cols), _BF16)),
        grid=(m // tm, n_j),
        in_specs=[pl.BlockSpec((tm, d), lambda i, j: (i, 0)),
                  pl.BlockSpec((d, tn), lambda i, j: (0, c0 + j)),
                  pl.BlockSpec((rb, cols), lambda i, j: (i * n_j + j, 0))],
        out_specs=(pl.BlockSpec((tm, tn), lambda i, j: (i, j)),
                   pl.BlockSpec((rb, cols), lambda i, j: (i * n_j + j, 0))),
        compiler_params=_params(("parallel", "arbitrary")),
        name="proj_gate",
    )(h, w_rest, w_cast)


def _matmul_kernel(a_ref, b_ref, o_ref):
    o_ref[...] = jnp.dot(a_ref[...], b_ref[...].astype(_BF16), preferred_element_type=_F32).astype(o_ref.dtype)


def _ctx_kv_call(hc, w_in):
    m, d = hc.shape
    tm, tn = 1024, 1024
    c0 = OFF_K // tn
    return pl.pallas_call(
        _matmul_kernel,
        out_shape=jax.ShapeDtypeStruct((m, 2 * KV_COLS), _BF16),
        grid=(m // tm,),
        in_specs=[pl.BlockSpec((tm, d), lambda i: (i, 0)),
                  pl.BlockSpec((d, tn), lambda i: (0, c0))],
        out_specs=pl.BlockSpec((tm, tn), lambda i: (i, 0)),
        compiler_params=_params(("parallel",)),
        name="ctx_kv",
    )(hc, w_in)


def _attn_kernel(sink_ref, q_ref, kp_ref, ko_ref, kn_ref, vp_ref, vo_ref, vn_ref, kc_ref, vc_ref, o_ref):
    i = pl.program_id(1)
    nb = pl.num_programs(1)
    rows = GROUP * BLOCK
    row = lax.broadcasted_iota(jnp.int32, (rows, BLOCK), 0)
    ii = jnp.bitwise_and(row, BLOCK - 1)
    jj = lax.broadcasted_iota(jnp.int32, (rows, BLOCK), 1)
    mask_p = jnp.logical_and(jj >= ii, i > 0)
    mask_n = jnp.logical_and(jj <= ii, i < nb - 1)
    row1 = lax.broadcasted_iota(jnp.int32, (rows, 1), 0)
    dn_t = (((1,), (1,)), ((), ()))
    for g in range(N_KV_HEADS):
        ks = slice(g * HEAD_DIM, (g + 1) * HEAD_DIM)
        q4 = jnp.concatenate(
            [q_ref[0, :, (g * GROUP + h) * HEAD_DIM:(g * GROUP + h + 1) * HEAD_DIM] for h in range(GROUP)], axis=0)
        k_all = jnp.concatenate([kp_ref[0, :, ks], ko_ref[0, :, ks], kn_ref[0, :, ks], kc_ref[0, :, ks]], axis=0)
        v_all = jnp.concatenate([vp_ref[0, :, ks], vo_ref[0, :, ks], vn_ref[0, :, ks], vc_ref[0, :, ks]], axis=0)
        s = lax.dot_general(q4, k_all, dn_t, preferred_element_type=_F32)
        s_p = jnp.where(mask_p, s[:, 0:BLOCK], NEG_INF)
        s_o = s[:, BLOCK:2 * BLOCK]
        s_n = jnp.where(mask_n, s[:, 2 * BLOCK:3 * BLOCK], NEG_INF)
        n_ctx = (s.shape[1] - 3 * BLOCK) // BLOCK
        s_c = [s[:, (3 + t) * BLOCK:(4 + t) * BLOCK] for t in range(n_ctx)]
        sink = jnp.full((rows, 1), sink_ref[g * GROUP + GROUP - 1] * LOG2E, _F32)
        for h in range(GROUP - 2, -1, -1):
            sink = jnp.where(row1 < (h + 1) * BLOCK, sink_ref[g * GROUP + h] * LOG2E, sink)
        mx = jnp.maximum(jnp.maximum(s_p, s_o), s_n)
        for t in s_c:
            mx = jnp.maximum(mx, t)
        m = jnp.maximum(jnp.max(mx, axis=-1, keepdims=True), sink)
        parts = [jnp.exp2(t - m) for t in [s_p, s_o, s_n] + s_c]
        tot = parts[0]
        for t in parts[1:]:
            tot = tot + t
        denom = jnp.sum(tot, axis=-1, keepdims=True) + jnp.exp2(sink - m)
        p = jnp.concatenate([t.astype(_BF16) for t in parts], axis=1)
        o = jnp.dot(p, v_all, preferred_element_type=_F32) / denom
        for h in range(GROUP):
            c0 = (g * GROUP + h) * HEAD_DIM
            o_ref[0, :, c0:c0 + HEAD_DIM] = o[h * BLOCK:(h + 1) * BLOCK, :].astype(o_ref.dtype)


def _attn_call(qkv, kvc, sink):
    b, s, _ = qkv.shape
    l = kvc.shape[1]
    nb = s // BLOCK
    kcol = OFF_K // KV_COLS
    vcol = OFF_V // KV_COLS
    kv_blk = (1, BLOCK, KV_COLS)
    return pl.pallas_call(
        _attn_kernel,
        out_shape=jax.ShapeDtypeStruct((b, s, Q_COLS), _BF16),
        grid=(b, nb),
        in_specs=[pl.BlockSpec(memory_space=pltpu.SMEM),
                  pl.BlockSpec((1, BLOCK, Q_COLS), lambda bi, i: (bi, i, 0)),
                  pl.BlockSpec(kv_blk, lambda bi, i: (bi, jnp.maximum(i - 1, 0), kcol)),
                  pl.BlockSpec(kv_blk, lambda bi, i: (bi, i, kcol)),
                  pl.BlockSpec(kv_blk, lambda bi, i: (bi, jnp.minimum(i + 1, nb - 1), kcol)),
                  pl.BlockSpec(kv_blk, lambda bi, i: (bi, jnp.maximum(i - 1, 0), vcol)),
                  pl.BlockSpec(kv_blk, lambda bi, i: (bi, i, vcol)),
                  pl.BlockSpec(kv_blk, lambda bi, i: (bi, jnp.minimum(i + 1, nb - 1), vcol)),
                  pl.BlockSpec((1, l, KV_COLS), lambda bi, i: (bi, 0, 0)),
                  pl.BlockSpec((1, l, KV_COLS), lambda bi, i: (bi, 0, 1))],
        out_specs=pl.BlockSpec((1, BLOCK, Q_COLS), lambda bi, i: (bi, i, 0)),
        compiler_params=_params(("parallel", "arbitrary")),
        name="attn",
    )(sink, qkv, qkv, qkv, qkv, qkv, qkv, qkv, kvc, kvc)


def _glu_conv_kernel(h_ref, wa_ref, wg_ref, cw_ref, cb_ref, c0_ref, c1_ref, c2_ref,
                     o_ref, c0_o_ref, c1_o_ref, c2_o_ref, win_even_ref, win_odd_ref, wb_ref, y_ref):
    t = pl.program_id(0)
    s, ct = h_ref.shape[0], wa_ref.shape[1]
    part = s // SUB
    n_lt = ct // LANES
    rc = 8
    first = CONV_HALO - CONV_K // 2

    c0_o_ref[...] = c0_ref[...].astype(c0_o_ref.dtype)
    c1_o_ref[...] = c1_ref[...].astype(c1_o_ref.dtype)
    c2_o_ref[...] = c2_ref[...].astype(c2_o_ref.dtype)

    @pl.when(t == 0)
    def _():
        win_odd_ref[...] = jnp.zeros(win_odd_ref.shape, _F32)

    def project(cur_ref):
        h = h_ref[...]
        a = jnp.dot(h, wa_ref[...], preferred_element_type=_F32)
        g = jnp.dot(h, wg_ref[...], preferred_element_type=_F32)
        u = a * jax.nn.sigmoid(g)
        zeros = jnp.zeros((CONV_HALO, LANES), _F32)
        for lt in range(n_lt):
            ls = slice(lt * LANES, (lt + 1) * LANES)
            for p in range(SUB):
                lo = u[p * part - CONV_HALO:p * part, ls] if p > 0 else zeros
                hi = u[(p + 1) * part:(p + 1) * part + CONV_HALO, ls] if p < SUB - 1 else zeros
                cur_ref[lt, pl.ds(p, CONV_HALO, stride=SUB), :] = lo
                cur_ref[lt, pl.ds(CONV_HALO * SUB + p, part, stride=SUB), :] = u[p * part:(p + 1) * part, ls]
                cur_ref[lt, pl.ds((CONV_HALO + part) * SUB + p, CONV_HALO, stride=SUB), :] = hi

    def convolve(prev_ref):
        for k in range(CONV_K):
            wb_ref[k * SUB:(k + 1) * SUB, :] = jnp.broadcast_to(cw_ref[k:k + 1, :], (SUB, ct))
        bias = jnp.broadcast_to(cb_ref[...], (SUB, ct))
        for c in range(part // rc):
            r0 = c * rc
            for lt in range(n_lt):
                ls = slice(lt * LANES, (lt + 1) * LANES)
                accs = [bias[:, ls] for _ in range(rc)]
                for k in range(CONV_K):
                    wk = wb_ref[k * SUB:(k + 1) * SUB, ls]
                    for j in range(rc):
                        q = (r0 + j + k + first) * SUB
                        accs[j] = accs[j] + prev_ref[lt, q:q + SUB, :] * wk
                for j in range(rc):
                    y_ref[lt, (r0 + j) * SUB:(r0 + j + 1) * SUB, :] = accs[j]
        for p in range(SUB):
            for lt in range(n_lt):
                o_ref[0, p * part:(p + 1) * part, lt * LANES:(lt + 1) * LANES] = y_ref[
                    lt, pl.ds(p, part, stride=SUB), :].astype(o_ref.dtype)

    @pl.when(t % 2 == 0)
    def _():
        project(win_even_ref)
        convolve(win_odd_ref)

    @pl.when(t % 2 == 1)
    def _():
        project(win_odd_ref)
        convolve(win_even_ref)


def _glu_conv_call(h, w_rest, conv_w, conv_b, w_casts, batch, seq):
    m, d = h.shape
    ct = 256
    part = seq // SUB
    n_ct = D_MODEL // ct
    n_tiles = batch * n_ct
    a0 = 0
    g0 = D_MODEL // ct
    rb = d // n_tiles

    def cur(t):
        return jnp.minimum(t, n_tiles - 1)

    def prev(t):
        return jnp.maximum(t - 1, 0)

    cast_spec = pl.BlockSpec((rb, d), lambda t: (cur(t), 0))
    win = pltpu.VMEM((ct // LANES, (part + 2 * CONV_HALO) * SUB, LANES), _F32)
    return pl.pallas_call(
        _glu_conv_kernel,
        out_shape=(jax.ShapeDtypeStruct((batch, seq, D_MODEL), _BF16),) + tuple(
            jax.ShapeDtypeStruct(w.shape, _BF16) for w in w_casts),
        grid=(n_tiles + 1,),
        in_specs=[pl.BlockSpec((seq, d), lambda t: (cur(t) // n_ct, 0)),
                  pl.BlockSpec((d, ct), lambda t: (0, a0 + cur(t) % n_ct)),
                  pl.BlockSpec((d, ct), lambda t: (0, g0 + cur(t) % n_ct)),
                  pl.BlockSpec((CONV_K, ct), lambda t: (0, prev(t) % n_ct)),
                  pl.BlockSpec((1, ct), lambda t: (0, prev(t) % n_ct)),
                  cast_spec, cast_spec, cast_spec],
        out_specs=(pl.BlockSpec((1, seq, ct), lambda t: (prev(t) // n_ct, 0, prev(t) % n_ct)),
                   cast_spec, cast_spec, cast_spec),
        scratch_shapes=[win, win,
                        pltpu.VMEM((CONV_K * SUB, ct), _F32),
                        pltpu.VMEM((ct // LANES, seq, LANES), _F32)],
        compiler_params=_params(("arbitrary",)),
        name="glu_conv",
    )(h, w_rest, w_rest, conv_w, conv_b, *w_casts)


def _ln_silu_kernel(y_ref, g_ref, b_ref, o_ref):
    z = _ln(y_ref[...].astype(_F32)) * g_ref[...] + b_ref[...]
    o_ref[...] = (z * jax.nn.sigmoid(z)).astype(o_ref.dtype)


def _ln_silu_call(y, norm_g, norm_b):
    m, d = y.shape
    tr = 512
    return pl.pallas_call(
        _ln_silu_kernel,
        out_shape=jax.ShapeDtypeStruct((m, d), _BF16),
        grid=(m // tr,),
        in_specs=[pl.BlockSpec((tr, d), lambda i: (i, 0)),
                  pl.BlockSpec((1, d), lambda i: (0, 0)),
                  pl.BlockSpec((1, d), lambda i: (0, 0))],
        out_specs=pl.BlockSpec((tr, d), lambda i: (i, 0)),
        compiler_params=_params(("parallel",)),
        name="ln_silu",
    )(y, norm_g, norm_b)


def _merge_kernel(attn_ref, yc_ref, ga_ref, gc_ref, wa_ref, wc_ref, o_ref):
    ya = jnp.dot(attn_ref[...], wa_ref[...], preferred_element_type=_F32)
    yc = jnp.dot(yc_ref[...], wc_ref[...], preferred_element_type=_F32)
    o_ref[...] = (ga_ref[...].astype(_F32) * ya + gc_ref[...].astype(_F32) * yc).astype(o_ref.dtype)


def _merge_call(attn, yc, gates, w_attn, w_conv):
    m, d = attn.shape
    tm, tn = 1024, 1024
    nj = d // tn
    return pl.pallas_call(
        _merge_kernel,
        out_shape=jax.ShapeDtypeStruct((m, d), _BF16),
        grid=(m // tm, nj),
        in_specs=[pl.BlockSpec((tm, d), lambda i, j: (i, 0)),
                  pl.BlockSpec((tm, d), lambda i, j: (i, 0)),
                  pl.BlockSpec((tm, tn), lambda i, j: (i, j)),
                  pl.BlockSpec((tm, tn), lambda i, j: (i, nj + j)),
                  pl.BlockSpec((d, tn), lambda i, j: (0, j)),
                  pl.BlockSpec((d, tn), lambda i, j: (0, j))],
        out_specs=pl.BlockSpec((tm, tn), lambda i, j: (i, j)),
        compiler_params=_params(("parallel", "arbitrary")),
        name="merge",
    )(attn, yc, gates, gates, w_attn, w_conv)


def _outproj_kernel(m_ref, x_ref, mod_ref, w_ref, g_ref, b_ref, cast_ref, x1_ref, h2_ref, cast_o_ref):
    cast_o_ref[...] = cast_ref[...].astype(cast_o_ref.dtype)
    mix = jnp.dot(m_ref[...], w_ref[...], preferred_element_type=_F32)
    gate1 = mod_ref[0, 2:3, :]
    shift2 = mod_ref[0, 3:4, :]
    scale2 = mod_ref[0, 4:5, :]
    r = ALPHA * _ln(x_ref[...]) + gate1 * mix
    x1 = _ln(r) * g_ref[...] + b_ref[...]
    x1_ref[...] = x1
    h2_ref[...] = (x1 * (1.0 + scale2) + shift2).astype(h2_ref.dtype)


def _outproj_call(mixed, x2d, mod, w_out, ln_g, ln_b, w_cast, seq):
    m, d = mixed.shape
    tm = 512
    tiles_per_batch = seq // tm
    rows, cols = w_cast.shape
    rb = rows // (m // tm)
    return pl.pallas_call(
        _outproj_kernel,
        out_shape=(jax.ShapeDtypeStruct((m, d), _F32), jax.ShapeDtypeStruct((m, d), _BF16),
                   jax.ShapeDtypeStruct((rows, cols), _BF16)),
        grid=(m // tm,),
        in_specs=[pl.BlockSpec((tm, d), lambda i: (i, 0)),
                  pl.BlockSpec((tm, d), lambda i: (i, 0)),
                  pl.BlockSpec((1, 6, d), lambda i: (i // tiles_per_batch, 0, 0)),
                  pl.BlockSpec((d, d), lambda i: (0, 0)),
                  pl.BlockSpec((1, d), lambda i: (0, 0)),
                  pl.BlockSpec((1, d), lambda i: (0, 0)),
                  pl.BlockSpec((rb, cols), lambda i: (i, 0))],
        out_specs=(pl.BlockSpec((tm, d), lambda i: (i, 0)),
                   pl.BlockSpec((tm, d), lambda i: (i, 0)),
                   pl.BlockSpec((rb, cols), lambda i: (i, 0))),
        compiler_params=_params(("parallel",)),
        name="outproj",
    )(mixed, x2d, mod, w_out, ln_g, ln_b, w_cast)


def _ffn_kernel(h_ref, x1_hbm_ref, mod_ref, wg_ref, wu_ref, wo_ref, g_ref, b_ref, o_ref, a_ref, x1_ref, x1_sem):
    i = pl.program_id(0)
    k = pl.program_id(1)
    tm = o_ref.shape[0]
    x1_copy = pltpu.make_async_copy(x1_hbm_ref.at[pl.ds(pl.multiple_of(i * tm, tm), tm)], x1_ref, x1_sem)

    @pl.when(k == 0)
    def _():
        x1_copy.start()
        o_ref[...] = jnp.zeros_like(o_ref)

    h = h_ref[...]
    nc = 256
    for n in range(wg_ref.shape[1] // nc):
        sl = slice(n * nc, (n + 1) * nc)
        g = jnp.dot(h, wg_ref[:, sl], preferred_element_type=_F32)
        u = jnp.dot(h, wu_ref[:, sl], preferred_element_type=_F32)
        a_ref[:, sl] = (g * jax.nn.sigmoid(g) * u).astype(a_ref.dtype)
    nc = 512
    for n in range(o_ref.shape[1] // nc):
        sl = slice(n * nc, (n + 1) * nc)
        o_ref[:, sl] += jnp.dot(a_ref[...], wo_ref[:, sl], preferred_element_type=_F32)

    @pl.when(k == pl.num_programs(1) - 1)
    def _():
        x1_copy.wait()
        gate2 = mod_ref[0, 5:6, :]
        r = ALPHA * x1_ref[...] + gate2 * o_ref[...]
        o_ref[...] = _ln(r) * g_ref[...] + b_ref[...]


def _ffn_call(h2, x1, mod, w_ffn_in, w_ffn_out, ln_g, ln_b, seq):
    m, d = h2.shape
    hidden = w_ffn_out.shape[0]
    tm, tk = 1024, 512
    nk = hidden // tk
    tiles_per_batch = seq // tm
    return pl.pallas_call(
        _ffn_kernel,
        out_shape=jax.ShapeDtypeStruct((m, d), _F32),
        grid=(m // tm, nk),
        in_specs=[pl.BlockSpec((tm, d), lambda i, k: (i, 0)),
                  pl.BlockSpec(memory_space=pl.ANY),
                  pl.BlockSpec((1, 6, d), lambda i, k: (i // tiles_per_batch, 0, 0)),
                  pl.BlockSpec((d, tk), lambda i, k: (0, k)),
                  pl.BlockSpec((d, tk), lambda i, k: (0, nk + k)),
                  pl.BlockSpec((tk, d), lambda i, k: (k, 0)),
                  pl.BlockSpec((1, d), lambda i, k: (0, 0)),
                  pl.BlockSpec((1, d), lambda i, k: (0, 0))],
        out_specs=pl.BlockSpec((tm, d), lambda i, k: (i, 0)),
        scratch_shapes=[pltpu.VMEM((tm, tk), _BF16),
                        pltpu.VMEM((tm, d), _F32),
                        pltpu.SemaphoreType.DMA(())],
        compiler_params=_params(("arbitrary", "arbitrary"), VMEM_LIMIT_MAX),
        name="ffn",
    )(h2, x1, mod, w_ffn_in, w_ffn_in, w_ffn_out, ln_g, ln_b)


def _rope_tables(seq):
    t = jnp.arange(seq, dtype=jnp.int32)
    n_freq = HEAD_DIM // 4
    inv_freq = ROPE_BASE ** (-jnp.arange(n_freq, dtype=_F32) / n_freq)
    ang_r = (t // GRID_W).astype(_F32)[:, None] * inv_freq
    ang_c = (t % GRID_W).astype(_F32)[:, None] * inv_freq
    zero = jnp.zeros_like(ang_r)
    cos = jnp.concatenate([jnp.cos(ang_r), jnp.cos(ang_r), jnp.cos(ang_c), jnp.cos(ang_c)], axis=1)
    sa = jnp.concatenate([-jnp.sin(ang_r), zero, -jnp.sin(ang_c), zero], axis=1)
    sb = jnp.concatenate([zero, jnp.sin(ang_r), zero, jnp.sin(ang_c)], axis=1)
    base = jnp.stack([cos, sa, sb])
    ident = jnp.stack([jnp.ones_like(cos), jnp.zeros_like(cos), jnp.zeros_like(cos)])
    return jnp.stack([base * (SCALE * LOG2E), base, ident])


def kernel(x, c, ctx, c_ctx, w_mod, b_mod, w_in, attn_sink, conv_w, conv_b, conv_norm_g, conv_norm_b,
           w_attn_proj, w_conv_proj, w_out, ln1_g, ln1_b, w_ffn_in, w_ffn_out, ln2_g, ln2_b):
    b, s, d = x.shape
    l = ctx.shape[1]
    m = b * s

    c_all = jnp.concatenate([c, c_ctx[None, :], jnp.zeros((16 - b - 1, d), _F32)], axis=0)
    mod_all = _mod_call(c_all, w_mod[0], b_mod[0][None, :])
    mod = mod_all[:b].reshape(b, 6, d)
    mod_c = mod_all[b:b + 1].reshape(1, 6, d)

    w_in0 = w_in[0]
    x2d = x.reshape(m, d)
    h = _ln_mod_call(x2d, mod, s, 0, 1, "ln_mod_x")
    hc = _ln_mod_call(ctx.reshape(b * l, d), mod_c, b * l, 0, 1, "ln_mod_ctx")

    tab = _rope_tables(s)
    qkv, w_rest = _qkv_call(h, w_in0, tab, s)
    qkv = qkv.reshape(b, s, OFF_GLU)
    kvc = _ctx_kv_call(hc, w_in0).reshape(b, l, 2 * KV_COLS)
    gates, w_ffn_in_b = _gate_call(h, w_rest, w_ffn_in[0])

    attn = _attn_call(qkv, kvc, attn_sink[0]).reshape(m, d)
    y, w_attn_b, w_conv_b, w_out_b = _glu_conv_call(
        h, w_rest, conv_w[0], conv_b[0][None, :], (w_attn_proj[0], w_conv_proj[0], w_out[0]), b, s)
    y = y.reshape(m, d)

    yc = _ln_silu_call(y, conv_norm_g[0][None, :], conv_norm_b[0][None, :])
    mixed = _merge_call(attn, yc, gates, w_attn_b, w_conv_b)
    x1, h2, w_ffn_out_b = _outproj_call(mixed, x2d, mod, w_out_b, ln1_g[0][None, :], ln1_b[0][None, :],
                                        w_ffn_out[0], s)
    out = _ffn_call(h2, x1, mod, w_ffn_in_b, w_ffn_out_b, ln2_g[0][None, :], ln2_b[0][None, :], s)
    return out.reshape(b, s, d)
```

```python
import functools

import jax
import jax.numpy as jnp
from jax import lax
from jax.experimental import pallas as pl
from jax.experimental.pallas import tpu as pltpu

D_MODEL = 2048
GRID_W = 64
HEAD_DIM = 128
N_HEADS = 16
N_KV_HEADS = 4
GROUP = N_HEADS // N_KV_HEADS
BLOCK = 128
CONV_K = 31
CONV_HALO = 16
PARTS = 16
LANES = 128
ROPE_BASE = 10000.0
DEPTH = 1
ALPHA = (2.0 * DEPTH) ** 0.25
EPS = 1e-6
NEG_INF = -1e30
SCALE = HEAD_DIM ** -0.5
LOG2E = 1.4426950408889634

Q_COLS = N_HEADS * HEAD_DIM
KV_COLS = N_KV_HEADS * HEAD_DIM
OFF_K = Q_COLS
OFF_V = OFF_K + KV_COLS
OFF_GLU = OFF_V + KV_COLS
OFF_GATE = OFF_GLU + 2 * D_MODEL

VMEM_LIMIT = 56 * 1024 * 1024
VMEM_LIMIT_MAX = 62 * 1024 * 1024

_BF16 = jnp.bfloat16
_F32 = jnp.float32


def _params(sem, vmem_limit=VMEM_LIMIT):
    return pltpu.CompilerParams(dimension_semantics=sem, vmem_limit_bytes=vmem_limit)


def _ln(xf):
    mu = jnp.mean(xf, axis=-1, keepdims=True)
    xc = xf - mu
    var = jnp.mean(xc * xc, axis=-1, keepdims=True)
    return xc * lax.rsqrt(var + EPS)


def _mod_kernel(c_ref, w_ref, b_ref, o_ref):
    c = c_ref[...]
    s = (c * jax.nn.sigmoid(c)).astype(_BF16)
    o_ref[...] = jnp.dot(s, w_ref[...].astype(_BF16), preferred_element_type=_F32) + b_ref[...]


def _mod_call(c_all, w_mod, b_mod):
    rows, d = c_all.shape
    n = w_mod.shape[1]
    tn = 1024
    return pl.pallas_call(
        _mod_kernel,
        out_shape=jax.ShapeDtypeStruct((rows, n), _F32),
        grid=(n // tn,),
        in_specs=[pl.BlockSpec((rows, d), lambda j: (0, 0)),
                  pl.BlockSpec((d, tn), lambda j: (0, j)),
                  pl.BlockSpec((1, tn), lambda j: (0, j))],
        out_specs=pl.BlockSpec((rows, tn), lambda j: (0, j)),
        compiler_params=_params(("arbitrary",)),
        name="mod",
    )(c_all, w_mod, b_mod)


def _ln_mod_kernel(x_ref, mod_ref, o_ref, *, shift_row, scale_row):
    y = _ln(x_ref[...])
    shift = mod_ref[0, shift_row:shift_row + 1, :]
    scale = mod_ref[0, scale_row:scale_row + 1, :]
    o_ref[...] = (y * (1.0 + scale) + shift).astype(o_ref.dtype)


def _ln_mod_call(x2d, mod, rows_per_group, shift_row, scale_row, name):
    r, d = x2d.shape
    tr = 512
    tiles_per_group = rows_per_group // tr
    return pl.pallas_call(
        functools.partial(_ln_mod_kernel, shift_row=shift_row, scale_row=scale_row),
        out_shape=jax.ShapeDtypeStruct((r, d), _BF16),
        grid=(r // tr,),
        in_specs=[pl.BlockSpec((tr, d), lambda i: (i, 0)),
                  pl.BlockSpec((1, 6, d), lambda i: (i // tiles_per_group, 0, 0))],
        out_specs=pl.BlockSpec((tr, d), lambda i: (i, 0)),
        compiler_params=_params(("parallel",)),
        name=name,
    )(x2d, mod)


def _rope(xs, c, sa, sb):
    return xs * c + pltpu.roll(xs, 96, 1) * sa + pltpu.roll(xs, 32, 1) * sb


def _qkv_kernel(h_ref, w_ref, tab_lo_ref, tab_hi_ref, rest_ref, o_ref, rest_o_ref, wb_ref):
    @pl.when(pl.program_id(1) == 0)
    def _():
        wb_ref[...] = w_ref[...].astype(wb_ref.dtype)

    rest_o_ref[...] = rest_ref[...].astype(rest_o_ref.dtype)

    acc = jnp.dot(h_ref[...], wb_ref[...], preferred_element_type=_F32)
    n_sl = acc.shape[1] // HEAD_DIM
    for hh in range(n_sl):
        tab_ref = tab_lo_ref if hh < n_sl // 2 else tab_hi_ref
        sl = slice(hh * HEAD_DIM, (hh + 1) * HEAD_DIM)
        o_ref[:, sl] = _rope(acc[:, sl], tab_ref[0, 0], tab_ref[0, 1], tab_ref[0, 2]).astype(o_ref.dtype)


def _qkv_call(h, w_in, tab, seq):
    m, d = h.shape
    tm, tn = 1024, 1024
    n_out = OFF_GLU
    n_i = m // tm
    t_blocks = seq // tm
    q_tiles = Q_COLS // tn
    tab_blk = (1, 3, tm, HEAD_DIM)
    rest_cols = w_in.shape[1] - OFF_GLU
    rb, cb = 512, 1024
    n_rb = d // rb
    n_blocks = n_rb * (rest_cols // cb)

    def rest_block(j, i):
        return jnp.minimum(j * n_i + i, n_blocks - 1)

    return pl.pallas_call(
        _qkv_kernel,
        out_shape=(jax.ShapeDtypeStruct((m, n_out), _BF16), jax.ShapeDtypeStruct((d, rest_cols), _BF16)),
        grid=(n_out // tn, n_i),
        in_specs=[pl.BlockSpec((tm, d), lambda j, i: (i, 0)),
                  pl.BlockSpec((d, tn), lambda j, i: (0, j)),
                  pl.BlockSpec(tab_blk, lambda j, i: (jnp.where(j < q_tiles, 0, 1), 0, i % t_blocks, 0)),
                  pl.BlockSpec(tab_blk, lambda j, i: (jnp.where(j < q_tiles, 0, 2), 0, i % t_blocks, 0)),
                  pl.BlockSpec((rb, cb), lambda j, i: (rest_block(j, i) % n_rb,
                                                       OFF_GLU // cb + rest_block(j, i) // n_rb))],
        out_specs=(pl.BlockSpec((tm, tn), lambda j, i: (i, j)),
                   pl.BlockSpec((rb, cb), lambda j, i: (rest_block(j, i) % n_rb, rest_block(j, i) // n_rb))),
        scratch_shapes=[pltpu.VMEM((d, tn), _BF16)],
        compiler_params=_params(("arbitrary", "arbitrary")),
        name="proj_qkv",
    )(h, w_in, tab, tab, w_in)


def _gate_kernel(h_ref, w_ref, cast_ref, o_ref, cast_o_ref):
    cast_o_ref[...] = cast_ref[...].astype(cast_o_ref.dtype)
    acc = jnp.dot(h_ref[...], w_ref[...], preferred_element_type=_F32)
    o_ref[...] = jax.nn.sigmoid(acc).astype(o_ref.dtype)


def _gate_call(h, w_rest, w_cast):
    m, d = h.shape
    tm, tn = 1024, 1024
    c0 = (OFF_GATE - OFF_GLU) // tn
    n_j = 2 * D_MODEL // tn
    rows, cols = w_cast.shape
    rb = rows // ((m // tm) * n_j)
    return pl.pallas_call(
        _gate_kernel,
        out_shape=(jax.ShapeDtypeStruct((m, 2 * D_MODEL), _BF16), jax.ShapeDtypeStruct((rows, cols), _BF16)),
        grid=(m // tm, n_j),
        in_specs=[pl.BlockSpec((tm, d), lambda i, j: (i, 0)),
                  pl.BlockSpec((d, tn), lambda i, j: (0, c0 + j)),
                  pl.BlockSpec((rb, cols), lambda i, j: (i * n_j + j, 0))],
        out_specs=(pl.BlockSpec((tm, tn), lambda i, j: (i, j)),
                   pl.BlockSpec((rb, cols), lambda i, j: (i * n_j + j, 0))),
        compiler_params=_params(("parallel", "arbitrary")),
        name="proj_gate",
    )(h, w_rest, w_cast)


def _matmul_kernel(a_ref, b_ref, o_ref):
    o_ref[...] = jnp.dot(a_ref[...], b_ref[...].astype(_BF16), preferred_element_type=_F32).astype(o_ref.dtype)


def _ctx_kv_call(hc, w_in):
    m, d = hc.shape
    tm, tn = 1024, 1024
    c0 = OFF_K // tn
    return pl.pallas_call(
        _matmul_kernel,
        out_shape=jax.ShapeDtypeStruct((m, 2 * KV_COLS), _BF16),
        grid=(m // tm,),
        in_specs=[pl.BlockSpec((tm, d), lambda i: (i, 0)),
                  pl.BlockSpec((d, tn), lambda i: (0, c0))],
        out_specs=pl.BlockSpec((tm, tn), lambda i: (i, 0)),
        compiler_params=_params(("parallel",)),
        name="ctx_kv",
    )(hc, w_in)


def _attn_kernel(sink_ref, q_ref, kp_ref, ko_ref, kn_ref, vp_ref, vo_ref, vn_ref, kc_ref, vc_ref, o_ref):
    i = pl.program_id(1)
    nb = pl.num_programs(1)
    rows = GROUP * BLOCK
    row = lax.broadcasted_iota(jnp.int32, (rows, BLOCK), 0)
    ii = jnp.bitwise_and(row, BLOCK - 1)
    jj = lax.broadcasted_iota(jnp.int32, (rows, BLOCK), 1)
    mask_p = jnp.logical_and(jj >= ii, i > 0)
    mask_n = jnp.logical_and(jj <= ii, i < nb - 1)
    row1 = lax.broadcasted_iota(jnp.int32, (rows, 1), 0)
    dn_t = (((1,), (1,)), ((), ()))
    for g in range(N_KV_HEADS):
        ks = slice(g * HEAD_DIM, (g + 1) * HEAD_DIM)
        q4 = jnp.concatenate(
            [q_ref[0, :, (g * GROUP + h) * HEAD_DIM:(g * GROUP + h + 1) * HEAD_DIM] for h in range(GROUP)], axis=0)
        k_all = jnp.concatenate([kp_ref[0, :, ks], ko_ref[0, :, ks], kn_ref[0, :, ks], kc_ref[0, :, ks]], axis=0)
        v_all = jnp.concatenate([vp_ref[0, :, ks], vo_ref[0, :, ks], vn_ref[0, :, ks], vc_ref[0, :, ks]], axis=0)
        s = lax.dot_general(q4, k_all, dn_t, preferred_element_type=_F32)
        s_p = jnp.where(mask_p, s[:, 0:BLOCK], NEG_INF)
        s_o = s[:, BLOCK:2 * BLOCK]
        s_n = jnp.where(mask_n, s[:, 2 * BLOCK:3 * BLOCK], NEG_INF)
        n_ctx = (s.shape[1] - 3 * BLOCK) // BLOCK
        s_c = [s[:, (3 + t) * BLOCK:(4 + t) * BLOCK] for t in range(n_ctx)]
        sink = jnp.full((rows, 1), sink_ref[g * GROUP + GROUP - 1] * LOG2E, _F32)
        for h in range(GROUP - 2, -1, -1):
            sink = jnp.where(row1 < (h + 1) * BLOCK, sink_ref[g * GROUP + h] * LOG2E, sink)
        mx = jnp.maximum(jnp.maximum(s_p, s_o), s_n)
        for t in s_c:
            mx = jnp.maximum(mx, t)
        m = jnp.maximum(jnp.max(mx, axis=-1, keepdims=True), sink)
        parts = [jnp.exp2(t - m) for t in [s_p, s_o, s_n] + s_c]
        tot = parts[0]
        for t in parts[1:]:
            tot = tot + t
        denom = jnp.sum(tot, axis=-1, keepdims=True) + jnp.exp2(sink - m)
        p = jnp.concatenate([t.astype(_BF16) for t in parts], axis=1)
        o = jnp.dot(p, v_all, preferred_element_type=_F32) / denom
        for h in range(GROUP):
            c0 = (g * GROUP + h) * HEAD_DIM
            o_ref[0, :, c0:c0 + HEAD_DIM] = o[h * BLOCK:(h + 1) * BLOCK, :].astype(o_ref.dtype)


def _attn_call(qkv, kvc, sink):
    b, s, _ = qkv.shape
    l = kvc.shape[1]
    nb = s // BLOCK
    kcol = OFF_K // KV_COLS
    vcol = OFF_V // KV_COLS
    kv_blk = (1, BLOCK, KV_COLS)
    return pl.pallas_call(
        _attn_kernel,
        out_shape=jax.ShapeDtypeStruct((b, s, Q_COLS), _BF16),
        grid=(b, nb),
        in_specs=[pl.BlockSpec(memory_space=pltpu.SMEM),
                  pl.BlockSpec((1, BLOCK, Q_COLS), lambda bi, i: (bi, i, 0)),
                  pl.BlockSpec(kv_blk, lambda bi, i: (bi, jnp.maximum(i - 1, 0), kcol)),
                  pl.BlockSpec(kv_blk, lambda bi, i: (bi, i, kcol)),
                  pl.BlockSpec(kv_blk, lambda bi, i: (bi, jnp.minimum(i + 1, nb - 1), kcol)),
                  pl.BlockSpec(kv_blk, lambda bi, i: (bi, jnp.maximum(i - 1, 0), vcol)),
                  pl.BlockSpec(kv_blk, lambda bi, i: (bi, i, vcol)),
                  pl.BlockSpec(kv_blk, lambda bi, i: (bi, jnp.minimum(i + 1, nb - 1), vcol)),
                  pl.BlockSpec((1, l, KV_COLS), lambda bi, i: (bi, 0, 0)),
                  pl.BlockSpec((1, l, KV_COLS), lambda bi, i: (bi, 0, 1))],
        out_specs=pl.BlockSpec((1, BLOCK, Q_COLS), lambda bi, i: (bi, i, 0)),
        compiler_params=_params(("parallel", "arbitrary")),
        name="attn",
    )(sink, qkv, qkv, qkv, qkv, qkv, qkv, qkv, kvc, kvc)


def _glu_conv_kernel(h_ref, wa_ref, wg_ref, cw_ref, cb_ref, c0_ref, c1_ref, c2_ref,
                     o_ref, c0_o_ref, c1_o_ref, c2_o_ref, stage_ref, win_ref, wb_ref, y_ref):
    s, ct = h_ref.shape[0], wa_ref.shape[1]
    part = s // PARTS
    n_lt = ct // LANES
    rc = 4
    first = CONV_HALO - CONV_K // 2

    c0_o_ref[...] = c0_ref[...].astype(c0_o_ref.dtype)
    c1_o_ref[...] = c1_ref[...].astype(c1_o_ref.dtype)
    c2_o_ref[...] = c2_ref[...].astype(c2_o_ref.dtype)

    h = h_ref[...]
    a = jnp.dot(h, wa_ref[...], preferred_element_type=_F32)
    g = jnp.dot(h, wg_ref[...], preferred_element_type=_F32)
    u = a * jax.nn.sigmoid(g)
    zeros = jnp.zeros((CONV_HALO, LANES), _F32)
    for lt in range(n_lt):
        ls = slice(lt * LANES, (lt + 1) * LANES)
        for p in range(PARTS):
            lo = u[p * part - CONV_HALO:p * part, ls] if p > 0 else zeros
            hi = u[(p + 1) * part:(p + 1) * part + CONV_HALO, ls] if p < PARTS - 1 else zeros
            stage_ref[lt, pl.ds(p, CONV_HALO, stride=PARTS), :] = lo
            stage_ref[lt, pl.ds(CONV_HALO * PARTS + p, part, stride=PARTS), :] = u[p * part:(p + 1) * part, ls]
            stage_ref[lt, pl.ds((CONV_HALO + part) * PARTS + p, CONV_HALO, stride=PARTS), :] = hi
    win_ref[...] = stage_ref[...].astype(win_ref.dtype)
    for k in range(CONV_K):
        wb_ref[k * PARTS:(k + 1) * PARTS, :] = jnp.broadcast_to(cw_ref[k:k + 1, :], (PARTS, ct)).astype(wb_ref.dtype)

    for lt in range(n_lt):
        ls = slice(lt * LANES, (lt + 1) * LANES)

        def body(c, carry, lt=lt, ls=ls):
            base = pl.multiple_of(c * (rc * PARTS), rc * PARTS)
            taps = win_ref.at[lt, pl.ds(base, (rc + CONV_K + first) * PARTS)]
            outs = y_ref.at[lt, pl.ds(base, rc * PARTS)]
            accs = [jnp.zeros((PARTS, LANES), _F32) for _ in range(rc)]
            for k in range(CONV_K):
                wk = wb_ref[k * PARTS:(k + 1) * PARTS, ls].astype(_F32)
                for j in range(rc):
                    q = (j + k + first) * PARTS
                    accs[j] = accs[j] + taps[q:q + PARTS, :].astype(_F32) * wk
            for j in range(rc):
                outs[j * PARTS:(j + 1) * PARTS, :] = accs[j]
            return carry

        lax.fori_loop(0, part // rc, body, 0)

    for p in range(PARTS):
        for lt in range(n_lt):
            ls = slice(lt * LANES, (lt + 1) * LANES)
            o_ref[0, p * part:(p + 1) * part, ls] = (
                y_ref[lt, pl.ds(p, part, stride=PARTS), :] + cb_ref[:, ls]).astype(o_ref.dtype)


def _glu_conv_call(h, w_rest, conv_w, conv_b, w_casts, batch, seq):
    m, d = h.shape
    ct = 256
    part = seq // PARTS
    n_ct = D_MODEL // ct
    n_tiles = batch * n_ct
    a0 = 0
    g0 = D_MODEL // ct
    rb = d // n_tiles
    win_rows = (part + 2 * CONV_HALO) * PARTS

    cast_spec = pl.BlockSpec((rb, d), lambda t: (t, 0))
    return pl.pallas_call(
        _glu_conv_kernel,
        out_shape=(jax.ShapeDtypeStruct((batch, seq, D_MODEL), _BF16),) + tuple(
            jax.ShapeDtypeStruct(w.shape, _BF16) for w in w_casts),
        grid=(n_tiles,),
        in_specs=[pl.BlockSpec((seq, d), lambda t: (t // n_ct, 0)),
                  pl.BlockSpec((d, ct), lambda t: (0, a0 + t % n_ct)),
                  pl.BlockSpec((d, ct), lambda t: (0, g0 + t % n_ct)),
                  pl.BlockSpec((CONV_K, ct), lambda t: (0, t % n_ct)),
                  pl.BlockSpec((1, ct), lambda t: (0, t % n_ct)),
                  cast_spec, cast_spec, cast_spec],
        out_specs=(pl.BlockSpec((1, seq, ct), lambda t: (t // n_ct, 0, t % n_ct)),
                   cast_spec, cast_spec, cast_spec),
        scratch_shapes=[pltpu.VMEM((ct // LANES, win_rows, LANES), _F32),
                        pltpu.VMEM((ct // LANES, win_rows, LANES), _BF16),
                        pltpu.VMEM((CONV_K * PARTS, ct), _BF16),
                        pltpu.VMEM((ct // LANES, seq, LANES), _F32)],
        compiler_params=_params(("arbitrary",)),
        name="glu_conv",
    )(h, w_rest, w_rest, conv_w, conv_b, *w_casts)


def _ln_silu_kernel(y_ref, g_ref, b_ref, o_ref):
    z = _ln(y_ref[...].astype(_F32)) * g_ref[...] + b_ref[...]
    o_ref[...] = (z * jax.nn.sigmoid(z)).astype(o_ref.dtype)


def _ln_silu_call(y, norm_g, norm_b):
    m, d = y.shape
    tr = 512
    return pl.pallas_call(
        _ln_silu_kernel,
        out_shape=jax.ShapeDtypeStruct((m, d), _BF16),
        grid=(m // tr,),
        in_specs=[pl.BlockSpec((tr, d), lambda i: (i, 0)),
                  pl.BlockSpec((1, d), lambda i: (0, 0)),
                  pl.BlockSpec((1, d), lambda i: (0, 0))],
        out_specs=pl.BlockSpec((tr, d), lambda i: (i, 0)),
        compiler_params=_params(("parallel",)),
        name="ln_silu",
    )(y, norm_g, norm_b)


def _merge_kernel(attn_ref, yc_ref, ga_ref, gc_ref, wa_ref, wc_ref, o_ref):
    ya = jnp.dot(attn_ref[...], wa_ref[...], preferred_element_type=_F32)
    yc = jnp.dot(yc_ref[...], wc_ref[...], preferred_element_type=_F32)
    o_ref[...] = (ga_ref[...].astype(_F32) * ya + gc_ref[...].astype(_F32) * yc).astype(o_ref.dtype)


def _merge_call(attn, yc, gates, w_attn, w_conv):
    m, d = attn.shape
    tm, tn = 1024, 1024
    nj = d // tn
    return pl.pallas_call(
        _merge_kernel,
        out_shape=jax.ShapeDtypeStruct((m, d), _BF16),
        grid=(m // tm, nj),
        in_specs=[pl.BlockSpec((tm, d), lambda i, j: (i, 0)),
                  pl.BlockSpec((tm, d), lambda i, j: (i, 0)),
                  pl.BlockSpec((tm, tn), lambda i, j: (i, j)),
                  pl.BlockSpec((tm, tn), lambda i, j: (i, nj + j)),
                  pl.BlockSpec((d, tn), lambda i, j: (0, j)),
                  pl.BlockSpec((d, tn), lambda i, j: (0, j))],
        out_specs=pl.BlockSpec((tm, tn), lambda i, j: (i, j)),
        compiler_params=_params(("parallel", "arbitrary")),
        name="merge",
    )(attn, yc, gates, gates, w_attn, w_conv)


def _outproj_kernel(m_ref, x_ref, mod_ref, w_ref, g_ref, b_ref, cast_ref, x1_ref, h2_ref, cast_o_ref):
    cast_o_ref[...] = cast_ref[...].astype(cast_o_ref.dtype)
    mix = jnp.dot(m_ref[...], w_ref[...], preferred_element_type=_F32)
    gate1 = mod_ref[0, 2:3, :]
    shift2 = mod_ref[0, 3:4, :]
    scale2 = mod_ref[0, 4:5, :]
    r = ALPHA * _ln(x_ref[...]) + gate1 * mix
    x1 = _ln(r) * g_ref[...] + b_ref[...]
    x1_ref[...] = x1
    h2_ref[...] = (x1 * (1.0 + scale2) + shift2).astype(h2_ref.dtype)


def _outproj_call(mixed, x2d, mod, w_out, ln_g, ln_b, w_cast, seq):
    m, d = mixed.shape
    tm = 512
    tiles_per_batch = seq // tm
    rows, cols = w_cast.shape
    rb = rows // (m // tm)
    return pl.pallas_call(
        _outproj_kernel,
        out_shape=(jax.ShapeDtypeStruct((m, d), _F32), jax.ShapeDtypeStruct((m, d), _BF16),
                   jax.ShapeDtypeStruct((rows, cols), _BF16)),
        grid=(m // tm,),
        in_specs=[pl.BlockSpec((tm, d), lambda i: (i, 0)),
                  pl.BlockSpec((tm, d), lambda i: (i, 0)),
                  pl.BlockSpec((1, 6, d), lambda i: (i // tiles_per_batch, 0, 0)),
                  pl.BlockSpec((d, d), lambda i: (0, 0)),
                  pl.BlockSpec((1, d), lambda i: (0, 0)),
                  pl.BlockSpec((1, d), lambda i: (0, 0)),
                  pl.BlockSpec((rb, cols), lambda i: (i, 0))],
        out_specs=(pl.BlockSpec((tm, d), lambda i: (i, 0)),
                   pl.BlockSpec((tm, d), lambda i: (i, 0)),
                   pl.BlockSpec((rb, cols), lambda i: (i, 0))),
        compiler_params=_params(("parallel",)),
        name="outproj",
    )(mixed, x2d, mod, w_out, ln_g, ln_b, w_cast)


def _ffn_kernel(h_ref, x1_hbm_ref, mod_ref, wg_ref, wu_ref, wo_ref, g_ref, b_ref, o_ref, a_ref, x1_ref, x1_sem):
    i = pl.program_id(0)
    k = pl.program_id(1)
    tm = o_ref.shape[0]
    x1_copy = pltpu.make_async_copy(x1_hbm_ref.at[pl.ds(pl.multiple_of(i * tm, tm), tm)], x1_ref, x1_sem)

    @pl.when(k == 0)
    def _():
        x1_copy.start()
        o_ref[...] = jnp.zeros_like(o_ref)

    h = h_ref[...]
    nc = 256
    for n in range(wg_ref.shape[1] // nc):
        sl = slice(n * nc, (n + 1) * nc)
        g = jnp.dot(h, wg_ref[:, sl], preferred_element_type=_F32)
        u = jnp.dot(h, wu_ref[:, sl], preferred_element_type=_F32)
        a_ref[:, sl] = (g * jax.nn.sigmoid(g) * u).astype(a_ref.dtype)
    nc = 512
    for n in range(o_ref.shape[1] // nc):
        sl = slice(n * nc, (n + 1) * nc)
        o_ref[:, sl] += jnp.dot(a_ref[...], wo_ref[:, sl], preferred_element_type=_F32)

    @pl.when(k == pl.num_programs(1) - 1)
    def _():
        x1_copy.wait()
        gate2 = mod_ref[0, 5:6, :]
        r = ALPHA * x1_ref[...] + gate2 * o_ref[...]
        o_ref[...] = _ln(r) * g_ref[...] + b_ref[...]


def _ffn_call(h2, x1, mod, w_ffn_in, w_ffn_out, ln_g, ln_b, seq):
    m, d = h2.shape
    hidden = w_ffn_out.shape[0]
    tm, tk = 1024, 512
    nk = hidden // tk
    tiles_per_batch = seq // tm
    return pl.pallas_call(
        _ffn_kernel,
        out_shape=jax.ShapeDtypeStruct((m, d), _F32),
        grid=(m // tm, nk),
        in_specs=[pl.BlockSpec((tm, d), lambda i, k: (i, 0)),
                  pl.BlockSpec(memory_space=pl.ANY),
                  pl.BlockSpec((1, 6, d), lambda i, k: (i // tiles_per_batch, 0, 0)),
                  pl.BlockSpec((d, tk), lambda i, k: (0, k)),
                  pl.BlockSpec((d, tk), lambda i, k: (0, nk + k)),
                  pl.BlockSpec((tk, d), lambda i, k: (k, 0)),
                  pl.BlockSpec((1, d), lambda i, k: (0, 0)),
                  pl.BlockSpec((1, d), lambda i, k: (0, 0))],
        out_specs=pl.BlockSpec((tm, d), lambda i, k: (i, 0)),
        scratch_shapes=[pltpu.VMEM((tm, tk), _BF16),
                        pltpu.VMEM((tm, d), _F32),
                        pltpu.SemaphoreType.DMA(())],
        compiler_params=_params(("arbitrary", "arbitrary"), VMEM_LIMIT_MAX),
        name="ffn",
    )(h2, x1, mod, w_ffn_in, w_ffn_in, w_ffn_out, ln_g, ln_b)


def _rope_tables(seq):
    t = jnp.arange(seq, dtype=jnp.int32)
    n_freq = HEAD_DIM // 4
    inv_freq = ROPE_BASE ** (-jnp.arange(n_freq, dtype=_F32) / n_freq)
    ang_r = (t // GRID_W).astype(_F32)[:, None] * inv_freq
    ang_c = (t % GRID_W).astype(_F32)[:, None] * inv_freq
    zero = jnp.zeros_like(ang_r)
    cos = jnp.concatenate([jnp.cos(ang_r), jnp.cos(ang_r), jnp.cos(ang_c), jnp.cos(ang_c)], axis=1)
    sa = jnp.concatenate([-jnp.sin(ang_r), zero, -jnp.sin(ang_c), zero], axis=1)
    sb = jnp.concatenate([zero, jnp.sin(ang_r), zero, jnp.sin(ang_c)], axis=1)
    base = jnp.stack([cos, sa, sb])
    ident = jnp.stack([jnp.ones_like(cos), jnp.zeros_like(cos), jnp.zeros_like(cos)])
    return jnp.stack([base * (SCALE * LOG2E), base, ident])


def kernel(x, c, ctx, c_ctx, w_mod, b_mod, w_in, attn_sink, conv_w, conv_b, conv_norm_g, conv_norm_b,
           w_attn_proj, w_conv_proj, w_out, ln1_g, ln1_b, w_ffn_in, w_ffn_out, ln2_g, ln2_b):
    b, s, d = x.shape
    l = ctx.shape[1]
    m = b * s

    c_all = jnp.concatenate([c, c_ctx[None, :], jnp.zeros((16 - b - 1, d), _F32)], axis=0)
    mod_all = _mod_call(c_all, w_mod[0], b_mod[0][None, :])
    mod = mod_all[:b].reshape(b, 6, d)
    mod_c = mod_all[b:b + 1].reshape(1, 6, d)

    w_in0 = w_in[0]
    x2d = x.reshape(m, d)
    h = _ln_mod_call(x2d, mod, s, 0, 1, "ln_mod_x")
    hc = _ln_mod_call(ctx.reshape(b * l, d), mod_c, b * l, 0, 1, "ln_mod_ctx")

    tab = _rope_tables(s)
    qkv, w_rest = _qkv_call(h, w_in0, tab, s)
    qkv = qkv.reshape(b, s, OFF_GLU)
    kvc = _ctx_kv_call(hc, w_in0).reshape(b, l, 2 * KV_COLS)
    gates, w_ffn_in_b = _gate_call(h, w_rest, w_ffn_in[0])

    attn = _attn_call(qkv, kvc, attn_sink[0]).reshape(m, d)
    y, w_attn_b, w_conv_b, w_out_b = _glu_conv_call(
        h, w_rest, conv_w[0], conv_b[0][None, :], (w_attn_proj[0], w_conv_proj[0], w_out[0]), b, s)
    y = y.reshape(m, d)

    yc = _ln_silu_call(y, conv_norm_g[0][None, :], conv_norm_b[0][None, :])
    mixed = _merge_call(attn, yc, gates, w_attn_b, w_conv_b)
    x1, h2, w_ffn_out_b = _outproj_call(mixed, x2d, mod, w_out_b, ln1_g[0][None, :], ln1_b[0][None, :],
                                        w_ffn_out[0], s)
    out = _ffn_call(h2, x1, mod, w_ffn_in_b, w_ffn_out_b, ln2_g[0][None, :], ln2_b[0][None, :], s)
    return out.reshape(b, s, d)
```

```python
import functools

import jax
import jax.numpy as jnp
from jax import lax
from jax.experimental import pallas as pl
from jax.experimental.pallas import tpu as pltpu

D_MODEL = 2048
GRID_W = 64
HEAD_DIM = 128
N_HEADS = 16
N_KV_HEADS = 4
GROUP = N_HEADS // N_KV_HEADS
BLOCK = 128
QB = 2
CONV_K = 31
CONV_HALO = 16
SUB = 8
LANES = 128
ROPE_BASE = 10000.0
DEPTH = 1
ALPHA = (2.0 * DEPTH) ** 0.25
EPS = 1e-6
NEG_INF = -1e30
SCALE = HEAD_DIM ** -0.5
LOG2E = 1.4426950408889634

Q_COLS = N_HEADS * HEAD_DIM
KV_COLS = N_KV_HEADS * HEAD_DIM
OFF_K = Q_COLS
OFF_V = OFF_K + KV_COLS
OFF_GLU = OFF_V + KV_COLS
OFF_GATE = OFF_GLU + 2 * D_MODEL

VMEM_LIMIT = 56 * 1024 * 1024
VMEM_LIMIT_MAX = 62 * 1024 * 1024

_BF16 = jnp.bfloat16
_F32 = jnp.float32


def _params(sem, vmem_limit=VMEM_LIMIT):
    return pltpu.CompilerParams(dimension_semantics=sem, vmem_limit_bytes=vmem_limit)


def _ln(xf):
    mu = jnp.mean(xf, axis=-1, keepdims=True)
    xc = xf - mu
    var = jnp.mean(xc * xc, axis=-1, keepdims=True)
    return xc * lax.rsqrt(var + EPS)


def _mod_kernel(c_ref, w_ref, b_ref, o_ref):
    c = c_ref[...]
    s = (c * jax.nn.sigmoid(c)).astype(_BF16)
    o_ref[...] = jnp.dot(s, w_ref[...].astype(_BF16), preferred_element_type=_F32) + b_ref[...]


def _mod_call(c_all, w_mod, b_mod):
    rows, d = c_all.shape
    n = w_mod.shape[1]
    tn = 1024
    return pl.pallas_call(
        _mod_kernel,
        out_shape=jax.ShapeDtypeStruct((rows, n), _F32),
        grid=(n // tn,),
        in_specs=[pl.BlockSpec((rows, d), lambda j: (0, 0)),
                  pl.BlockSpec((d, tn), lambda j: (0, j)),
                  pl.BlockSpec((1, tn), lambda j: (0, j))],
        out_specs=pl.BlockSpec((rows, tn), lambda j: (0, j)),
        compiler_params=_params(("arbitrary",)),
        name="mod",
    )(c_all, w_mod, b_mod)


def _ln_mod_kernel(x_ref, mod_ref, o_ref, *, shift_row, scale_row):
    y = _ln(x_ref[...])
    shift = mod_ref[0, shift_row:shift_row + 1, :]
    scale = mod_ref[0, scale_row:scale_row + 1, :]
    o_ref[...] = (y * (1.0 + scale) + shift).astype(o_ref.dtype)


def _ln_mod_call(x2d, mod, rows_per_group, shift_row, scale_row, name):
    r, d = x2d.shape
    tr = 1024
    tiles_per_group = rows_per_group // tr
    return pl.pallas_call(
        functools.partial(_ln_mod_kernel, shift_row=shift_row, scale_row=scale_row),
        out_shape=jax.ShapeDtypeStruct((r, d), _BF16),
        grid=(r // tr,),
        in_specs=[pl.BlockSpec((tr, d), lambda i: (i, 0)),
                  pl.BlockSpec((1, 6, d), lambda i: (i // tiles_per_group, 0, 0))],
        out_specs=pl.BlockSpec((tr, d), lambda i: (i, 0)),
        compiler_params=_params(("parallel",)),
        name=name,
    )(x2d, mod)


def _rope(xs, c, sa, sb):
    return xs * c + pltpu.roll(xs, 96, 1) * sa + pltpu.roll(xs, 32, 1) * sb


def _qkv_kernel(h_ref, w_ref, tab_lo_ref, tab_hi_ref, rest_ref, o_ref, rest_o_ref, wb_ref):
    @pl.when(pl.program_id(1) == 0)
    def _():
        wb_ref[...] = w_ref[...].astype(wb_ref.dtype)

    rest_o_ref[...] = rest_ref[...].astype(rest_o_ref.dtype)

    acc = jnp.dot(h_ref[...], wb_ref[...], preferred_element_type=_F32)
    n_sl = acc.shape[1] // HEAD_DIM
    for hh in range(n_sl):
        tab_ref = tab_lo_ref if hh < n_sl // 2 else tab_hi_ref
        sl = slice(hh * HEAD_DIM, (hh + 1) * HEAD_DIM)
        o_ref[:, sl] = _rope(acc[:, sl], tab_ref[0, 0], tab_ref[0, 1], tab_ref[0, 2]).astype(o_ref.dtype)


def _qkv_call(h, w_in, tab, seq):
    m, d = h.shape
    tm, tn = 1024, 1024
    n_out = OFF_GLU
    n_i = m // tm
    t_blocks = seq // tm
    q_tiles = Q_COLS // tn
    tab_blk = (1, 3, tm, HEAD_DIM)
    rest_cols = w_in.shape[1] - OFF_GLU
    rb, cb = 512, 1024
    n_rb = d // rb
    n_blocks = n_rb * (rest_cols // cb)

    def rest_block(j, i):
        return jnp.minimum(j * n_i + i, n_blocks - 1)

    return pl.pallas_call(
        _qkv_kernel,
        out_shape=(jax.ShapeDtypeStruct((m, n_out), _BF16), jax.ShapeDtypeStruct((d, rest_cols), _BF16)),
        grid=(n_out // tn, n_i),
        in_specs=[pl.BlockSpec((tm, d), lambda j, i: (i, 0)),
                  pl.BlockSpec((d, tn), lambda j, i: (0, j)),
                  pl.BlockSpec(tab_blk, lambda j, i: (jnp.where(j < q_tiles, 0, 1), 0, i % t_blocks, 0)),
                  pl.BlockSpec(tab_blk, lambda j, i: (jnp.where(j < q_tiles, 0, 2), 0, i % t_blocks, 0)),
                  pl.BlockSpec((rb, cb), lambda j, i: (rest_block(j, i) % n_rb,
                                                       OFF_GLU // cb + rest_block(j, i) // n_rb))],
        out_specs=(pl.BlockSpec((tm, tn), lambda j, i: (i, j)),
                   pl.BlockSpec((rb, cb), lambda j, i: (rest_block(j, i) % n_rb, rest_block(j, i) // n_rb))),
        scratch_shapes=[pltpu.VMEM((d, tn), _BF16)],
        compiler_params=_params(("arbitrary", "arbitrary")),
        name="proj_qkv",
    )(h, w_in, tab, tab, w_in)


def _gate_kernel(h_ref, w_ref, cast_ref, o_ref, cast_o_ref):
    cast_o_ref[...] = cast_ref[...].astype(cast_o_ref.dtype)
    acc = jnp.dot(h_ref[...], w_ref[...], preferred_element_type=_F32)
    o_ref[...] = jax.nn.sigmoid(acc).astype(o_ref.dtype)


def _gate_call(h, w_rest, w_cast):
    m, d = h.shape
    tm, tn = 1024, 1024
    c0 = (OFF_GATE - OFF_GLU) // tn
    n_j = 2 * D_MODEL // tn
    rows, cols = w_cast.shape
    rb = rows // ((m // tm) * n_j)
    return pl.pallas_call(
        _gate_kernel,
        out_shape=(jax.ShapeDtypeStruct((m, 2 * D_MODEL), _BF16), jax.ShapeDtypeStruct((rows, cols), _BF16)),
        grid=(m // tm, n_j),
        in_specs=[pl.BlockSpec((tm, d), lambda i, j: (i, 0)),
                  pl.BlockSpec((d, tn), lambda i, j: (0, c0 + j)),
                  pl.BlockSpec((rb, cols), lambda i, j: (i * n_j + j, 0))],
        out_specs=(pl.BlockSpec((tm, tn), lambda i, j: (i, j)),
                   pl.BlockSpec((rb, cols), lambda i, j: (i * n_j + j, 0))),
        compiler_params=_params(("parallel", "arbitrary")),
        name="proj_gate",
    )(h, w_rest, w_cast)


def _matmul_kernel(a_ref, b_ref, o_ref):
    o_ref[...] = jnp.dot(a_ref[...], b_ref[...].astype(_BF16), preferred_element_type=_F32).astype(o_ref.dtype)


def _ctx_kv_call(hc, w_in):
    m, d = hc.shape
    tm, tn = 1024, 1024
    c0 = OFF_K // tn
    return pl.pallas_call(
        _matmul_kernel,
        out_shape=jax.ShapeDtypeStruct((m, 2 * KV_COLS), _BF16),
        grid=(m // tm,),
        in_specs=[pl.BlockSpec((tm, d), lambda i: (i, 0)),
                  pl.BlockSpec((d, tn), lambda i: (0, c0))],
        out_specs=pl.BlockSpec((tm, tn), lambda i: (i, 0)),
        compiler_params=_params(("parallel",)),
        name="ctx_kv",
    )(hc, w_in)


def _attn_kernel(sink_ref, q_ref, kp_ref, ko_ref, kn_ref, vp_ref, vo_ref, vn_ref, kc_ref, vc_ref, o_ref):
    i = pl.program_id(1)
    n_steps = pl.num_programs(1)
    rows = GROUP * BLOCK
    row = lax.broadcasted_iota(jnp.int32, (rows, BLOCK), 0)
    ii = jnp.bitwise_and(row, BLOCK - 1)
    jj = lax.broadcasted_iota(jnp.int32, (rows, BLOCK), 1)
    lower = jj >= ii
    upper = jj <= ii
    row1 = lax.broadcasted_iota(jnp.int32, (rows, 1), 0)
    dn_t = (((1,), (1,)), ((), ()))
    for qb in range(QB):
        rs = slice(qb * BLOCK, (qb + 1) * BLOCK)
        mask_p = jnp.logical_and(lower, i > 0) if qb == 0 else lower
        mask_n = jnp.logical_and(upper, i < n_steps - 1) if qb == QB - 1 else upper
        for g in range(N_KV_HEADS):
            ks = slice(g * HEAD_DIM, (g + 1) * HEAD_DIM)

            def blocks(p_ref, o_blk_ref, n_ref, c_ref):
                prev = p_ref[0, :, ks] if qb == 0 else o_blk_ref[0, (qb - 1) * BLOCK:qb * BLOCK, ks]
                nxt = n_ref[0, :, ks] if qb == QB - 1 else o_blk_ref[0, (qb + 1) * BLOCK:(qb + 2) * BLOCK, ks]
                return jnp.concatenate([prev, o_blk_ref[0, rs, ks], nxt, c_ref[0, :, ks]], axis=0)

            q4 = jnp.concatenate(
                [q_ref[0, rs, (g * GROUP + h) * HEAD_DIM:(g * GROUP + h + 1) * HEAD_DIM] for h in range(GROUP)],
                axis=0)
            k_all = blocks(kp_ref, ko_ref, kn_ref, kc_ref)
            v_all = blocks(vp_ref, vo_ref, vn_ref, vc_ref)
            s = lax.dot_general(q4, k_all, dn_t, preferred_element_type=_F32)
            s_p = jnp.where(mask_p, s[:, 0:BLOCK], NEG_INF)
            s_o = s[:, BLOCK:2 * BLOCK]
            s_n = jnp.where(mask_n, s[:, 2 * BLOCK:3 * BLOCK], NEG_INF)
            n_ctx = (s.shape[1] - 3 * BLOCK) // BLOCK
            s_c = [s[:, (3 + t) * BLOCK:(4 + t) * BLOCK] for t in range(n_ctx)]
            sink = jnp.full((rows, 1), sink_ref[g * GROUP + GROUP - 1] * LOG2E, _F32)
            for h in range(GROUP - 2, -1, -1):
                sink = jnp.where(row1 < (h + 1) * BLOCK, sink_ref[g * GROUP + h] * LOG2E, sink)
            mx = jnp.maximum(jnp.maximum(s_p, s_o), s_n)
            for t in s_c:
                mx = jnp.maximum(mx, t)
            m = jnp.maximum(jnp.max(mx, axis=-1, keepdims=True), sink)
            parts = [jnp.exp2(t - m) for t in [s_p, s_o, s_n] + s_c]
            tot = parts[0]
            for t in parts[1:]:
                tot = tot + t
            denom = jnp.sum(tot, axis=-1, keepdims=True) + jnp.exp2(sink - m)
            p = jnp.concatenate([t.astype(_BF16) for t in parts], axis=1)
            o = jnp.dot(p, v_all, preferred_element_type=_F32) / denom
            for h in range(GROUP):
                c0 = (g * GROUP + h) * HEAD_DIM
                o_ref[0, rs, c0:c0 + HEAD_DIM] = o[h * BLOCK:(h + 1) * BLOCK, :].astype(o_ref.dtype)


def _attn_call(qkv, kvc, sink):
    b, s, _ = qkv.shape
    l = kvc.shape[1]
    nb = s // BLOCK
    n_steps = nb // QB
    kcol = OFF_K // KV_COLS
    vcol = OFF_V // KV_COLS
    edge_blk = (1, BLOCK, KV_COLS)
    own_blk = (1, QB * BLOCK, KV_COLS)

    def prev_idx(i):
        return jnp.maximum(QB * i - 1, 0)

    def next_idx(i):
        return jnp.minimum(QB * i + QB, nb - 1)

    return pl.pallas_call(
        _attn_kernel,
        out_shape=jax.ShapeDtypeStruct((b, s, Q_COLS), _BF16),
        grid=(b, n_steps),
        in_specs=[pl.BlockSpec(memory_space=pltpu.SMEM),
                  pl.BlockSpec((1, QB * BLOCK, Q_COLS), lambda bi, i: (bi, i, 0)),
                  pl.BlockSpec(edge_blk, lambda bi, i: (bi, prev_idx(i), kcol)),
                  pl.BlockSpec(own_blk, lambda bi, i: (bi, i, kcol)),
                  pl.BlockSpec(edge_blk, lambda bi, i: (bi, next_idx(i), kcol)),
                  pl.BlockSpec(edge_blk, lambda bi, i: (bi, prev_idx(i), vcol)),
                  pl.BlockSpec(own_blk, lambda bi, i: (bi, i, vcol)),
                  pl.BlockSpec(edge_blk, lambda bi, i: (bi, next_idx(i), vcol)),
                  pl.BlockSpec((1, l, KV_COLS), lambda bi, i: (bi, 0, 0)),
                  pl.BlockSpec((1, l, KV_COLS), lambda bi, i: (bi, 0, 1))],
        out_specs=pl.BlockSpec((1, QB * BLOCK, Q_COLS), lambda bi, i: (bi, i, 0)),
        compiler_params=_params(("parallel", "arbitrary")),
        name="attn",
    )(sink, qkv, qkv, qkv, qkv, qkv, qkv, qkv, kvc, kvc)


def _glu_conv_kernel(h_ref, wa_ref, wg_ref, cw_ref, cb_ref, c0_ref, c1_ref, c2_ref,
                     o_ref, c0_o_ref, c1_o_ref, c2_o_ref, win_even_ref, win_odd_ref, wb_ref, y_ref):
    t = pl.program_id(0)
    s, ct = h_ref.shape[0], wa_ref.shape[1]
    part = s // SUB
    n_lt = ct // LANES
    rc = 8
    first = CONV_HALO - CONV_K // 2

    c0_o_ref[...] = c0_ref[...].astype(c0_o_ref.dtype)
    c1_o_ref[...] = c1_ref[...].astype(c1_o_ref.dtype)
    c2_o_ref[...] = c2_ref[...].astype(c2_o_ref.dtype)

    @pl.when(t == 0)
    def _():
        win_odd_ref[...] = jnp.zeros(win_odd_ref.shape, _F32)

    def project(cur_ref):
        h = h_ref[...]
        a = jnp.dot(h, wa_ref[...], preferred_element_type=_F32)
        g = jnp.dot(h, wg_ref[...], preferred_element_type=_F32)
        u = a * jax.nn.sigmoid(g)
        zeros = jnp.zeros((CONV_HALO, LANES), _F32)
        for lt in range(n_lt):
            ls = slice(lt * LANES, (lt + 1) * LANES)
            for p in range(SUB):
                lo = u[p * part - CONV_HALO:p * part, ls] if p > 0 else zeros
                hi = u[(p + 1) * part:(p + 1) * part + CONV_HALO, ls] if p < SUB - 1 else zeros
                cur_ref[lt, pl.ds(p, CONV_HALO, stride=SUB), :] = lo
                cur_ref[lt, pl.ds(CONV_HALO * SUB + p, part, stride=SUB), :] = u[p * part:(p + 1) * part, ls]
                cur_ref[lt, pl.ds((CONV_HALO + part) * SUB + p, CONV_HALO, stride=SUB), :] = hi

    def convolve(prev_ref):
        for k in range(CONV_K):
            wb_ref[k * SUB:(k + 1) * SUB, :] = jnp.broadcast_to(cw_ref[k:k + 1, :], (SUB, ct))
        bias = jnp.broadcast_to(cb_ref[...], (SUB, ct))
        for c in range(part // rc):
            r0 = c * rc
            for lt in range(n_lt):
                ls = slice(lt * LANES, (lt + 1) * LANES)
                accs = [bias[:, ls] for _ in range(rc)]
                for k in range(CONV_K):
                    wk = wb_ref[k * SUB:(k + 1) * SUB, ls]
                    for j in range(rc):
                        q = (r0 + j + k + first) * SUB
                        accs[j] = accs[j] + prev_ref[lt, q:q + SUB, :] * wk
                for j in range(rc):
                    y_ref[lt, (r0 + j) * SUB:(r0 + j + 1) * SUB, :] = accs[j]
        for p in range(SUB):
            for lt in range(n_lt):
                o_ref[0, p * part:(p + 1) * part, lt * LANES:(lt + 1) * LANES] = y_ref[
                    lt, pl.ds(p, part, stride=SUB), :].astype(o_ref.dtype)

    @pl.when(t % 2 == 0)
    def _():
        project(win_even_ref)
        convolve(win_odd_ref)

    @pl.when(t % 2 == 1)
    def _():
        project(win_odd_ref)
        convolve(win_even_ref)


def _glu_conv_call(h, w_rest, conv_w, conv_b, w_casts, batch, seq):
    m, d = h.shape
    ct = 256
    part = seq // SUB
    n_ct = D_MODEL // ct
    n_tiles = batch * n_ct
    a0 = 0
    g0 = D_MODEL // ct
    rb = d // n_tiles

    def cur(t):
        return jnp.minimum(t, n_tiles - 1)

    def prev(t):
        return jnp.maximum(t - 1, 0)

    cast_spec = pl.BlockSpec((rb, d), lambda t: (cur(t), 0))
    win = pltpu.VMEM((ct // LANES, (part + 2 * CONV_HALO) * SUB, LANES), _F32)
    return pl.pallas_call(
        _glu_conv_kernel,
        out_shape=(jax.ShapeDtypeStruct((batch, seq, D_MODEL), _BF16),) + tuple(
            jax.ShapeDtypeStruct(w.shape, _BF16) for w in w_casts),
        grid=(n_tiles + 1,),
        in_specs=[pl.BlockSpec((seq, d), lambda t: (cur(t) // n_ct, 0)),
                  pl.BlockSpec((d, ct), lambda t: (0, a0 + cur(t) % n_ct)),
                  pl.BlockSpec((d, ct), lambda t: (0, g0 + cur(t) % n_ct)),
                  pl.BlockSpec((CONV_K, ct), lambda t: (0, prev(t) % n_ct)),
                  pl.BlockSpec((1, ct), lambda t: (0, prev(t) % n_ct)),
                  cast_spec, cast_spec, cast_spec],
        out_specs=(pl.BlockSpec((1, seq, ct), lambda t: (prev(t) // n_ct, 0, prev(t) % n_ct)),
                   cast_spec, cast_spec, cast_spec),
        scratch_shapes=[win, win,
                        pltpu.VMEM((CONV_K * SUB, ct), _F32),
                        pltpu.VMEM((ct // LANES, seq, LANES), _F32)],
        compiler_params=_params(("arbitrary",)),
        name="glu_conv",
    )(h, w_rest, w_rest, conv_w, conv_b, *w_casts)


def _ln_silu_kernel(y_ref, g_ref, b_ref, o_ref):
    z = _ln(y_ref[...].astype(_F32)) * g_ref[...] + b_ref[...]
    o_ref[...] = (z * jax.nn.sigmoid(z)).astype(o_ref.dtype)


def _ln_silu_call(y, norm_g, norm_b):
    m, d = y.shape
    tr = 1024
    return pl.pallas_call(
        _ln_silu_kernel,
        out_shape=jax.ShapeDtypeStruct((m, d), _BF16),
        grid=(m // tr,),
        in_specs=[pl.BlockSpec((tr, d), lambda i: (i, 0)),
                  pl.BlockSpec((1, d), lambda i: (0, 0)),
                  pl.BlockSpec((1, d), lambda i: (0, 0))],
        out_specs=pl.BlockSpec((tr, d), lambda i: (i, 0)),
        compiler_params=_params(("parallel",)),
        name="ln_silu",
    )(y, norm_g, norm_b)


def _merge_kernel(attn_ref, yc_ref, ga_ref, gc_ref, wa_ref, wc_ref, o_ref):
    ya = jnp.dot(attn_ref[...], wa_ref[...], preferred_element_type=_F32)
    yc = jnp.dot(yc_ref[...], wc_ref[...], preferred_element_type=_F32)
    o_ref[...] = (ga_ref[...].astype(_F32) * ya + gc_ref[...].astype(_F32) * yc).astype(o_ref.dtype)


def _merge_call(attn, yc, gates, w_attn, w_conv):
    m, d = attn.shape
    tm, tn = 1024, 1024
    nj = d // tn
    return pl.pallas_call(
        _merge_kernel,
        out_shape=jax.ShapeDtypeStruct((m, d), _BF16),
        grid=(m // tm, nj),
        in_specs=[pl.BlockSpec((tm, d), lambda i, j: (i, 0)),
                  pl.BlockSpec((tm, d), lambda i, j: (i, 0)),
                  pl.BlockSpec((tm, tn), lambda i, j: (i, j)),
                  pl.BlockSpec((tm, tn), lambda i, j: (i, nj + j)),
                  pl.BlockSpec((d, tn), lambda i, j: (0, j)),
                  pl.BlockSpec((d, tn), lambda i, j: (0, j))],
        out_specs=pl.BlockSpec((tm, tn), lambda i, j: (i, j)),
        compiler_params=_params(("parallel", "arbitrary")),
        name="merge",
    )(attn, yc, gates, gates, w_attn, w_conv)


def _outproj_kernel(m_ref, x_ref, mod_ref, w_ref, g_ref, b_ref, cast_ref, x1_ref, h2_ref, cast_o_ref):
    cast_o_ref[...] = cast_ref[...].astype(cast_o_ref.dtype)
    mix = jnp.dot(m_ref[...], w_ref[...], preferred_element_type=_F32)
    gate1 = mod_ref[0, 2:3, :]
    shift2 = mod_ref[0, 3:4, :]
    scale2 = mod_ref[0, 4:5, :]
    r = ALPHA * _ln(x_ref[...]) + gate1 * mix
    x1 = _ln(r) * g_ref[...] + b_ref[...]
    x1_ref[...] = x1
    h2_ref[...] = (x1 * (1.0 + scale2) + shift2).astype(h2_ref.dtype)


def _outproj_call(mixed, x2d, mod, w_out, ln_g, ln_b, w_cast, seq):
    m, d = mixed.shape
    tm = 512
    tiles_per_batch = seq // tm
    rows, cols = w_cast.shape
    rb = rows // (m // tm)
    return pl.pallas_call(
        _outproj_kernel,
        out_shape=(jax.ShapeDtypeStruct((m, d), _F32), jax.ShapeDtypeStruct((m, d), _BF16),
                   jax.ShapeDtypeStruct((rows, cols), _BF16)),
        grid=(m // tm,),
        in_specs=[pl.BlockSpec((tm, d), lambda i: (i, 0)),
                  pl.BlockSpec((tm, d), lambda i: (i, 0)),
                  pl.BlockSpec((1, 6, d), lambda i: (i // tiles_per_batch, 0, 0)),
                  pl.BlockSpec((d, d), lambda i: (0, 0)),
                  pl.BlockSpec((1, d), lambda i: (0, 0)),
                  pl.BlockSpec((1, d), lambda i: (0, 0)),
                  pl.BlockSpec((rb, cols), lambda i: (i, 0))],
        out_specs=(pl.BlockSpec((tm, d), lambda i: (i, 0)),
                   pl.BlockSpec((tm, d), lambda i: (i, 0)),
                   pl.BlockSpec((rb, cols), lambda i: (i, 0))),
        compiler_params=_params(("parallel",)),
        name="outproj",
    )(mixed, x2d, mod, w_out, ln_g, ln_b, w_cast)


def _ffn_kernel(h_ref, x1_hbm_ref, mod_ref, wg_ref, wu_ref, wo_ref, g_ref, b_ref, o_ref, a_ref, x1_ref, x1_sem):
    i = pl.program_id(0)
    k = pl.program_id(1)
    tm = o_ref.shape[0]
    x1_copy = pltpu.make_async_copy(x1_hbm_ref.at[pl.ds(pl.multiple_of(i * tm, tm), tm)], x1_ref, x1_sem)

    @pl.when(k == 0)
    def _():
        x1_copy.start()
        o_ref[...] = jnp.zeros_like(o_ref)

    h = h_ref[...]
    nc = 256
    for n in range(wg_ref.shape[1] // nc):
        sl = slice(n * nc, (n + 1) * nc)
        g = jnp.dot(h, wg_ref[:, sl], preferred_element_type=_F32)
        u = jnp.dot(h, wu_ref[:, sl], preferred_element_type=_F32)
        a_ref[:, sl] = (g * jax.nn.sigmoid(g) * u).astype(a_ref.dtype)
    nc = 512
    for n in range(o_ref.shape[1] // nc):
        sl = slice(n * nc, (n + 1) * nc)
        o_ref[:, sl] += jnp.dot(a_ref[...], wo_ref[:, sl], preferred_element_type=_F32)

    @pl.when(k == pl.num_programs(1) - 1)
    def _():
        x1_copy.wait()
        gate2 = mod_ref[0, 5:6, :]
        r = ALPHA * x1_ref[...] + gate2 * o_ref[...]
        o_ref[...] = _ln(r) * g_ref[...] + b_ref[...]


def _ffn_call(h2, x1, mod, w_ffn_in, w_ffn_out, ln_g, ln_b, seq):
    m, d = h2.shape
    hidden = w_ffn_out.shape[0]
    tm, tk = 1024, 512
    nk = hidden // tk
    tiles_per_batch = seq // tm
    return pl.pallas_call(
        _ffn_kernel,
        out_shape=jax.ShapeDtypeStruct((m, d), _F32),
        grid=(m // tm, nk),
        in_specs=[pl.BlockSpec((tm, d), lambda i, k: (i, 0)),
                  pl.BlockSpec(memory_space=pl.ANY),
                  pl.BlockSpec((1, 6, d), lambda i, k: (i // tiles_per_batch, 0, 0)),
                  pl.BlockSpec((d, tk), lambda i, k: (0, k)),
                  pl.BlockSpec((d, tk), lambda i, k: (0, nk + k)),
                  pl.BlockSpec((tk, d), lambda i, k: (k, 0)),
                  pl.BlockSpec((1, d), lambda i, k: (0, 0)),
                  pl.BlockSpec((1, d), lambda i, k: (0, 0))],
        out_specs=pl.BlockSpec((tm, d), lambda i, k: (i, 0)),
        scratch_shapes=[pltpu.VMEM((tm, tk), _BF16),
                        pltpu.VMEM((tm, d), _F32),
                        pltpu.SemaphoreType.DMA(())],
        compiler_params=_params(("arbitrary", "arbitrary"), VMEM_LIMIT_MAX),
        name="ffn",
    )(h2, x1, mod, w_ffn_in, w_ffn_in, w_ffn_out, ln_g, ln_b)


def _rope_tables(seq):
    t = jnp.arange(seq, dtype=jnp.int32)
    n_freq = HEAD_DIM // 4
    inv_freq = ROPE_BASE ** (-jnp.arange(n_freq, dtype=_F32) / n_freq)
    ang_r = (t // GRID_W).astype(_F32)[:, None] * inv_freq
    ang_c = (t % GRID_W).astype(_F32)[:, None] * inv_freq
    zero = jnp.zeros_like(ang_r)
    cos = jnp.concatenate([jnp.cos(ang_r), jnp.cos(ang_r), jnp.cos(ang_c), jnp.cos(ang_c)], axis=1)
    sa = jnp.concatenate([-jnp.sin(ang_r), zero, -jnp.sin(ang_c), zero], axis=1)
    sb = jnp.concatenate([zero, jnp.sin(ang_r), zero, jnp.sin(ang_c)], axis=1)
    base = jnp.stack([cos, sa, sb])
    ident = jnp.stack([jnp.ones_like(cos), jnp.zeros_like(cos), jnp.zeros_like(cos)])
    return jnp.stack([base * (SCALE * LOG2E), base, ident])


def kernel(x, c, ctx, c_ctx, w_mod, b_mod, w_in, attn_sink, conv_w, conv_b, conv_norm_g, conv_norm_b,
           w_attn_proj, w_conv_proj, w_out, ln1_g, ln1_b, w_ffn_in, w_ffn_out, ln2_g, ln2_b):
    b, s, d = x.shape
    l = ctx.shape[1]
    m = b * s

    c_all = jnp.concatenate([c, c_ctx[None, :], jnp.zeros((16 - b - 1, d), _F32)], axis=0)
    mod_all = _mod_call(c_all, w_mod[0], b_mod[0][None, :])
    mod = mod_all[:b].reshape(b, 6, d)
    mod_c = mod_all[b:b + 1].reshape(1, 6, d)

    w_in0 = w_in[0]
    x2d = x.reshape(m, d)
    h = _ln_mod_call(x2d, mod, s, 0, 1, "ln_mod_x")
    hc = _ln_mod_call(ctx.reshape(b * l, d), mod_c, b * l, 0, 1, "ln_mod_ctx")

    tab = _rope_tables(s)
    qkv, w_rest = _qkv_call(h, w_in0, tab, s)
    qkv = qkv.reshape(b, s, OFF_GLU)
    kvc = _ctx_kv_call(hc, w_in0).reshape(b, l, 2 * KV_COLS)
    gates, w_ffn_in_b = _gate_call(h, w_rest, w_ffn_in[0])

    attn = _attn_call(qkv, kvc, attn_sink[0]).reshape(m, d)
    y, w_attn_b, w_conv_b, w_out_b = _glu_conv_call(
        h, w_rest, conv_w[0], conv_b[0][None, :], (w_attn_proj[0], w_conv_proj[0], w_out[0]), b, s)
    y = y.reshape(m, d)

    yc = _ln_silu_call(y, conv_norm_g[0][None, :], conv_norm_b[0][None, :])
    mixed = _merge_call(attn, yc, gates, w_attn_b, w_conv_b)
    x1, h2, w_ffn_out_b = _outproj_call(mixed, x2d, mod, w_out_b, ln1_g[0][None, :], ln1_b[0][None, :],
                                        w_ffn_out[0], s)
    out = _ffn_call(h2, x1, mod, w_ffn_in_b, w_ffn_out_b, ln2_g[0][None, :], ln2_b[0][None, :], s)
    return out.reshape(b, s, d)
```

```python
import functools

import jax
import jax.numpy as jnp
from jax import lax
from jax.experimental import pallas as pl
from jax.experimental.pallas import tpu as pltpu

D_MODEL = 2048
GRID_W = 64
HEAD_DIM = 128
N_HEADS = 16
N_KV_HEADS = 4
GROUP = N_HEADS // N_KV_HEADS
BLOCK = 128
QB = 4
CONV_K = 31
CONV_HALO = 16
SUB = 8
LANES = 128
ROPE_BASE = 10000.0
DEPTH = 1
ALPHA = (2.0 * DEPTH) ** 0.25
EPS = 1e-6
NEG_INF = -1e30
SCALE = HEAD_DIM ** -0.5
LOG2E = 1.4426950408889634

Q_COLS = N_HEADS * HEAD_DIM
KV_COLS = N_KV_HEADS * HEAD_DIM
OFF_K = Q_COLS
OFF_V = OFF_K + KV_COLS
OFF_GLU = OFF_V + KV_COLS
OFF_GATE = OFF_GLU + 2 * D_MODEL

VMEM_LIMIT = 56 * 1024 * 1024
VMEM_LIMIT_MAX = 62 * 1024 * 1024

_BF16 = jnp.bfloat16
_F32 = jnp.float32


def _params(sem, vmem_limit=VMEM_LIMIT):
    return pltpu.CompilerParams(dimension_semantics=sem, vmem_limit_bytes=vmem_limit)


def _sigmoid(v):
    return 0.5 * jnp.tanh(0.5 * v) + 0.5


def _ln(xf):
    mu = jnp.mean(xf, axis=-1, keepdims=True)
    xc = xf - mu
    var = jnp.mean(xc * xc, axis=-1, keepdims=True)
    return xc * lax.rsqrt(var + EPS)


def _mod_kernel(c_ref, w_ref, b_ref, o_ref):
    c = c_ref[...]
    s = (c * _sigmoid(c)).astype(_BF16)
    o_ref[...] = jnp.dot(s, w_ref[...].astype(_BF16), preferred_element_type=_F32) + b_ref[...]


def _mod_call(c_all, w_mod, b_mod):
    rows, d = c_all.shape
    n = w_mod.shape[1]
    tn = 1024
    return pl.pallas_call(
        _mod_kernel,
        out_shape=jax.ShapeDtypeStruct((rows, n), _F32),
        grid=(n // tn,),
        in_specs=[pl.BlockSpec((rows, d), lambda j: (0, 0)),
                  pl.BlockSpec((d, tn), lambda j: (0, j)),
                  pl.BlockSpec((1, tn), lambda j: (0, j))],
        out_specs=pl.BlockSpec((rows, tn), lambda j: (0, j)),
        compiler_params=_params(("arbitrary",)),
        name="mod",
    )(c_all, w_mod, b_mod)


def _ln_mod_kernel(x_ref, mod_ref, o_ref, *, shift_row, scale_row):
    y = _ln(x_ref[...])
    shift = mod_ref[0, shift_row:shift_row + 1, :]
    scale = mod_ref[0, scale_row:scale_row + 1, :]
    o_ref[...] = (y * (1.0 + scale) + shift).astype(o_ref.dtype)


def _ln_mod_call(x2d, mod, rows_per_group, shift_row, scale_row, name):
    r, d = x2d.shape
    tr = 1024
    tiles_per_group = rows_per_group // tr
    return pl.pallas_call(
        functools.partial(_ln_mod_kernel, shift_row=shift_row, scale_row=scale_row),
        out_shape=jax.ShapeDtypeStruct((r, d), _BF16),
        grid=(r // tr,),
        in_specs=[pl.BlockSpec((tr, d), lambda i: (i, 0)),
                  pl.BlockSpec((1, 6, d), lambda i: (i // tiles_per_group, 0, 0))],
        out_specs=pl.BlockSpec((tr, d), lambda i: (i, 0)),
        compiler_params=_params(("parallel",)),
        name=name,
    )(x2d, mod)


def _rope(xs, c, sa, sb):
    return xs * c + pltpu.roll(xs, 96, 1) * sa + pltpu.roll(xs, 32, 1) * sb


def _qkv_kernel(h_ref, w_ref, tab_lo_ref, tab_hi_ref, rest_ref, o_ref, rest_o_ref, wb_ref):
    @pl.when(pl.program_id(1) == 0)
    def _():
        wb_ref[...] = w_ref[...].astype(wb_ref.dtype)

    rest_o_ref[...] = rest_ref[...].astype(rest_o_ref.dtype)

    acc = jnp.dot(h_ref[...], wb_ref[...], preferred_element_type=_F32)
    n_sl = acc.shape[1] // HEAD_DIM
    for hh in range(n_sl):
        tab_ref = tab_lo_ref if hh < n_sl // 2 else tab_hi_ref
        sl = slice(hh * HEAD_DIM, (hh + 1) * HEAD_DIM)
        o_ref[:, sl] = _rope(acc[:, sl], tab_ref[0, 0], tab_ref[0, 1], tab_ref[0, 2]).astype(o_ref.dtype)


def _qkv_call(h, w_in, tab, seq):
    m, d = h.shape
    tm, tn = 1024, 1024
    n_out = OFF_GLU
    n_i = m // tm
    t_blocks = seq // tm
    q_tiles = Q_COLS // tn
    tab_blk = (1, 3, tm, HEAD_DIM)
    rest_cols = w_in.shape[1] - OFF_GLU
    rb, cb = 512, 1024
    n_rb = d // rb
    n_blocks = n_rb * (rest_cols // cb)

    def rest_block(j, i):
        return jnp.minimum(j * n_i + i, n_blocks - 1)

    return pl.pallas_call(
        _qkv_kernel,
        out_shape=(jax.ShapeDtypeStruct((m, n_out), _BF16), jax.ShapeDtypeStruct((d, rest_cols), _BF16)),
        grid=(n_out // tn, n_i),
        in_specs=[pl.BlockSpec((tm, d), lambda j, i: (i, 0)),
                  pl.BlockSpec((d, tn), lambda j, i: (0, j)),
                  pl.BlockSpec(tab_blk, lambda j, i: (jnp.where(j < q_tiles, 0, 1), 0, i % t_blocks, 0)),
                  pl.BlockSpec(tab_blk, lambda j, i: (jnp.where(j < q_tiles, 0, 2), 0, i % t_blocks, 0)),
                  pl.BlockSpec((rb, cb), lambda j, i: (rest_block(j, i) % n_rb,
                                                       OFF_GLU // cb + rest_block(j, i) // n_rb))],
        out_specs=(pl.BlockSpec((tm, tn), lambda j, i: (i, j)),
                   pl.BlockSpec((rb, cb), lambda j, i: (rest_block(j, i) % n_rb, rest_block(j, i) // n_rb))),
        scratch_shapes=[pltpu.VMEM((d, tn), _BF16)],
        compiler_params=_params(("arbitrary", "arbitrary")),
        name="proj_qkv",
    )(h, w_in, tab, tab, w_in)


def _gate_kernel(h_ref, w_ref, cast_ref, o_ref, cast_o_ref):
    cast_o_ref[...] = cast_ref[...].astype(cast_o_ref.dtype)
    acc = jnp.dot(h_ref[...], w_ref[...], preferred_element_type=_F32)
    o_ref[...] = _sigmoid(acc).astype(o_ref.dtype)


def _gate_call(h, w_rest, w_cast):
    m, d = h.shape
    tm, tn = 1024, 1024
    c0 = (OFF_GATE - OFF_GLU) // tn
    n_j = 2 * D_MODEL // tn
    rows, cols = w_cast.shape
    rb = rows // ((m // tm) * n_j)
    return pl.pallas_call(
        _gate_kernel,
        out_shape=(jax.ShapeDtypeStruct((m, 2 * D_MODEL), _BF16), jax.ShapeDtypeStruct((rows, cols), _BF16)),
        grid=(m // tm, n_j),
        in_specs=[pl.BlockSpec((tm, d), lambda i, j: (i, 0)),
                  pl.BlockSpec((d, tn), lambda i, j: (0, c0 + j)),
                  pl.BlockSpec((rb, cols), lambda i, j: (i * n_j + j, 0))],
        out_specs=(pl.BlockSpec((tm, tn), lambda i, j: (i, j)),
                   pl.BlockSpec((rb, cols), lambda i, j: (i * n_j + j, 0))),
        compiler_params=_params(("parallel", "arbitrary")),
        name="proj_gate",
    )(h, w_rest, w_cast)


def _matmul_kernel(a_ref, b_ref, o_ref):
    o_ref[...] = jnp.dot(a_ref[...], b_ref[...].astype(_BF16), preferred_element_type=_F32).astype(o_ref.dtype)


def _ctx_kv_call(hc, w_in):
    m, d = hc.shape
    tm, tn = 1024, 1024
    c0 = OFF_K // tn
    return pl.pallas_call(
        _matmul_kernel,
        out_shape=jax.ShapeDtypeStruct((m, 2 * KV_COLS), _BF16),
        grid=(m // tm,),
        in_specs=[pl.BlockSpec((tm, d), lambda i: (i, 0)),
                  pl.BlockSpec((d, tn), lambda i: (0, c0))],
        out_specs=pl.BlockSpec((tm, tn), lambda i: (i, 0)),
        compiler_params=_params(("parallel",)),
        name="ctx_kv",
    )(hc, w_in)


def _attn_kernel(sink_ref, q_ref, kp_ref, ko_ref, kn_ref, vp_ref, vo_ref, vn_ref, kc_ref, vc_ref, o_ref):
    i = pl.program_id(1)
    n_steps = pl.num_programs(1)
    rows = GROUP * BLOCK
    row = lax.broadcasted_iota(jnp.int32, (rows, BLOCK), 0)
    ii = jnp.bitwise_and(row, BLOCK - 1)
    jj = lax.broadcasted_iota(jnp.int32, (rows, BLOCK), 1)
    lower = jj >= ii
    upper = jj <= ii
    row1 = lax.broadcasted_iota(jnp.int32, (rows, 1), 0)
    dn_t = (((1,), (1,)), ((), ()))
    for qb in range(QB):
        rs = slice(qb * BLOCK, (qb + 1) * BLOCK)
        mask_p = jnp.logical_and(lower, i > 0) if qb == 0 else lower
        mask_n = jnp.logical_and(upper, i < n_steps - 1) if qb == QB - 1 else upper
        for g in range(N_KV_HEADS):
            ks = slice(g * HEAD_DIM, (g + 1) * HEAD_DIM)

            def blocks(p_ref, o_blk_ref, n_ref, c_ref):
                prev = p_ref[0, :, ks] if qb == 0 else o_blk_ref[0, (qb - 1) * BLOCK:qb * BLOCK, ks]
                nxt = n_ref[0, :, ks] if qb == QB - 1 else o_blk_ref[0, (qb + 1) * BLOCK:(qb + 2) * BLOCK, ks]
                return jnp.concatenate([prev, o_blk_ref[0, rs, ks], nxt, c_ref[0, :, ks]], axis=0)

            q4 = jnp.concatenate(
                [q_ref[0, rs, (g * GROUP + h) * HEAD_DIM:(g * GROUP + h + 1) * HEAD_DIM] for h in range(GROUP)],
                axis=0)
            k_all = blocks(kp_ref, ko_ref, kn_ref, kc_ref)
            v_all = blocks(vp_ref, vo_ref, vn_ref, vc_ref)
            s = lax.dot_general(q4, k_all, dn_t, preferred_element_type=_F32)
            s_p = jnp.where(mask_p, s[:, 0:BLOCK], NEG_INF)
            s_o = s[:, BLOCK:2 * BLOCK]
            s_n = jnp.where(mask_n, s[:, 2 * BLOCK:3 * BLOCK], NEG_INF)
            n_ctx = (s.shape[1] - 3 * BLOCK) // BLOCK
            s_c = [s[:, (3 + t) * BLOCK:(4 + t) * BLOCK] for t in range(n_ctx)]
            sink = jnp.full((rows, 1), sink_ref[g * GROUP + GROUP - 1] * LOG2E, _F32)
            for h in range(GROUP - 2, -1, -1):
                sink = jnp.where(row1 < (h + 1) * BLOCK, sink_ref[g * GROUP + h] * LOG2E, sink)
            mx = jnp.maximum(jnp.maximum(s_p, s_o), s_n)
            for t in s_c:
                mx = jnp.maximum(mx, t)
            m = jnp.maximum(jnp.max(mx, axis=-1, keepdims=True), sink)
            parts = [jnp.exp2(t - m) for t in [s_p, s_o, s_n] + s_c]
            tot = parts[0]
            for t in parts[1:]:
                tot = tot + t
            denom = jnp.sum(tot, axis=-1, keepdims=True) + jnp.exp2(sink - m)
            p = jnp.concatenate([t.astype(_BF16) for t in parts], axis=1)
            o = jnp.dot(p, v_all, preferred_element_type=_F32) / denom
            for h in range(GROUP):
                c0 = (g * GROUP + h) * HEAD_DIM
                o_ref[0, rs, c0:c0 + HEAD_DIM] = o[h * BLOCK:(h + 1) * BLOCK, :].astype(o_ref.dtype)


def _attn_call(qkv, kvc, sink):
    b, s, _ = qkv.shape
    l = kvc.shape[1]
    nb = s // BLOCK
    n_steps = nb // QB
    kcol = OFF_K // KV_COLS
    vcol = OFF_V // KV_COLS
    edge_blk = (1, BLOCK, KV_COLS)
    own_blk = (1, QB * BLOCK, KV_COLS)

    def prev_idx(i):
        return jnp.maximum(QB * i - 1, 0)

    def next_idx(i):
        return jnp.minimum(QB * i + QB, nb - 1)

    return pl.pallas_call(
        _attn_kernel,
        out_shape=jax.ShapeDtypeStruct((b, s, Q_COLS), _BF16),
        grid=(b, n_steps),
        in_specs=[pl.BlockSpec(memory_space=pltpu.SMEM),
                  pl.BlockSpec((1, QB * BLOCK, Q_COLS), lambda bi, i: (bi, i, 0)),
                  pl.BlockSpec(edge_blk, lambda bi, i: (bi, prev_idx(i), kcol)),
                  pl.BlockSpec(own_blk, lambda bi, i: (bi, i, kcol)),
                  pl.BlockSpec(edge_blk, lambda bi, i: (bi, next_idx(i), kcol)),
                  pl.BlockSpec(edge_blk, lambda bi, i: (bi, prev_idx(i), vcol)),
                  pl.BlockSpec(own_blk, lambda bi, i: (bi, i, vcol)),
                  pl.BlockSpec(edge_blk, lambda bi, i: (bi, next_idx(i), vcol)),
                  pl.BlockSpec((1, l, KV_COLS), lambda bi, i: (bi, 0, 0)),
                  pl.BlockSpec((1, l, KV_COLS), lambda bi, i: (bi, 0, 1))],
        out_specs=pl.BlockSpec((1, QB * BLOCK, Q_COLS), lambda bi, i: (bi, i, 0)),
        compiler_params=_params(("parallel", "arbitrary")),
        name="attn",
    )(sink, qkv, qkv, qkv, qkv, qkv, qkv, qkv, kvc, kvc)


def _glu_conv_kernel(h_ref, wa_ref, wg_ref, cw_ref, cb_ref, c0_ref, c1_ref, c2_ref,
                     o_ref, c0_o_ref, c1_o_ref, c2_o_ref, win_even_ref, win_odd_ref, wb_ref, y_ref):
    t = pl.program_id(0)
    s, ct = h_ref.shape[0], wa_ref.shape[1]
    part = s // SUB
    n_lt = ct // LANES
    rc = 8
    first = CONV_HALO - CONV_K // 2

    c0_o_ref[...] = c0_ref[...].astype(c0_o_ref.dtype)
    c1_o_ref[...] = c1_ref[...].astype(c1_o_ref.dtype)
    c2_o_ref[...] = c2_ref[...].astype(c2_o_ref.dtype)

    @pl.when(t == 0)
    def _():
        win_odd_ref[...] = jnp.zeros(win_odd_ref.shape, _F32)

    def project(cur_ref):
        h = h_ref[...]
        a = jnp.dot(h, wa_ref[...], preferred_element_type=_F32)
        g = jnp.dot(h, wg_ref[...], preferred_element_type=_F32)
        u = a * _sigmoid(g)
        zeros = jnp.zeros((CONV_HALO, LANES), _F32)
        for lt in range(n_lt):
            ls = slice(lt * LANES, (lt + 1) * LANES)
            for p in range(SUB):
                lo = u[p * part - CONV_HALO:p * part, ls] if p > 0 else zeros
                hi = u[(p + 1) * part:(p + 1) * part + CONV_HALO, ls] if p < SUB - 1 else zeros
                cur_ref[lt, pl.ds(p, CONV_HALO, stride=SUB), :] = lo
                cur_ref[lt, pl.ds(CONV_HALO * SUB + p, part, stride=SUB), :] = u[p * part:(p + 1) * part, ls]
                cur_ref[lt, pl.ds((CONV_HALO + part) * SUB + p, CONV_HALO, stride=SUB), :] = hi

    def convolve(prev_ref):
        for k in range(CONV_K):
            wb_ref[k * SUB:(k + 1) * SUB, :] = jnp.broadcast_to(cw_ref[k:k + 1, :], (SUB, ct))
        bias = jnp.broadcast_to(cb_ref[...], (SUB, ct))
        for c in range(part // rc):
            r0 = c * rc
            for lt in range(n_lt):
                ls = slice(lt * LANES, (lt + 1) * LANES)
                accs = [bias[:, ls] for _ in range(rc)]
                for k in range(CONV_K):
                    wk = wb_ref[k * SUB:(k + 1) * SUB, ls]
                    for j in range(rc):
                        q = (r0 + j + k + first) * SUB
                        accs[j] = accs[j] + prev_ref[lt, q:q + SUB, :] * wk
                for j in range(rc):
                    y_ref[lt, (r0 + j) * SUB:(r0 + j + 1) * SUB, :] = accs[j]
        for p in range(SUB):
            for lt in range(n_lt):
                o_ref[0, p * part:(p + 1) * part, lt * LANES:(lt + 1) * LANES] = y_ref[
                    lt, pl.ds(p, part, stride=SUB), :].astype(o_ref.dtype)

    @pl.when(t % 2 == 0)
    def _():
        project(win_even_ref)
        convolve(win_odd_ref)

    @pl.when(t % 2 == 1)
    def _():
        project(win_odd_ref)
        convolve(win_even_ref)


def _glu_conv_call(h, w_rest, conv_w, conv_b, w_casts, batch, seq):
    m, d = h.shape
    ct = 256
    part = seq // SUB
    n_ct = D_MODEL // ct
    n_tiles = batch * n_ct
    a0 = 0
    g0 = D_MODEL // ct
    rb = d // n_tiles

    def cur(t):
        return jnp.minimum(t, n_tiles - 1)

    def prev(t):
        return jnp.maximum(t - 1, 0)

    cast_spec = pl.BlockSpec((rb, d), lambda t: (cur(t), 0))
    win = pltpu.VMEM((ct // LANES, (part + 2 * CONV_HALO) * SUB, LANES), _F32)
    return pl.pallas_call(
        _glu_conv_kernel,
        out_shape=(jax.ShapeDtypeStruct((batch, seq, D_MODEL), _BF16),) + tuple(
            jax.ShapeDtypeStruct(w.shape, _BF16) for w in w_casts),
        grid=(n_tiles + 1,),
        in_specs=[pl.BlockSpec((seq, d), lambda t: (cur(t) // n_ct, 0)),
                  pl.BlockSpec((d, ct), lambda t: (0, a0 + cur(t) % n_ct)),
                  pl.BlockSpec((d, ct), lambda t: (0, g0 + cur(t) % n_ct)),
                  pl.BlockSpec((CONV_K, ct), lambda t: (0, prev(t) % n_ct)),
                  pl.BlockSpec((1, ct), lambda t: (0, prev(t) % n_ct)),
                  cast_spec, cast_spec, cast_spec],
        out_specs=(pl.BlockSpec((1, seq, ct), lambda t: (prev(t) // n_ct, 0, prev(t) % n_ct)),
                   cast_spec, cast_spec, cast_spec),
        scratch_shapes=[win, win,
                        pltpu.VMEM((CONV_K * SUB, ct), _F32),
                        pltpu.VMEM((ct // LANES, seq, LANES), _F32)],
        compiler_params=_params(("arbitrary",)),
        name="glu_conv",
    )(h, w_rest, w_rest, conv_w, conv_b, *w_casts)


def _ln_silu_kernel(y_ref, g_ref, b_ref, o_ref):
    z = _ln(y_ref[...].astype(_F32)) * g_ref[...] + b_ref[...]
    o_ref[...] = (z * _sigmoid(z)).astype(o_ref.dtype)


def _ln_silu_call(y, norm_g, norm_b):
    m, d = y.shape
    tr = 1024
    return pl.pallas_call(
        _ln_silu_kernel,
        out_shape=jax.ShapeDtypeStruct((m, d), _BF16),
        grid=(m // tr,),
        in_specs=[pl.BlockSpec((tr, d), lambda i: (i, 0)),
                  pl.BlockSpec((1, d), lambda i: (0, 0)),
                  pl.BlockSpec((1, d), lambda i: (0, 0))],
        out_specs=pl.BlockSpec((tr, d), lambda i: (i, 0)),
        compiler_params=_params(("parallel",)),
        name="ln_silu",
    )(y, norm_g, norm_b)


def _merge_kernel(attn_ref, yc_ref, ga_ref, gc_ref, wa_ref, wc_ref, o_ref):
    ya = jnp.dot(attn_ref[...], wa_ref[...], preferred_element_type=_F32)
    yc = jnp.dot(yc_ref[...], wc_ref[...], preferred_element_type=_F32)
    o_ref[...] = (ga_ref[...].astype(_F32) * ya + gc_ref[...].astype(_F32) * yc).astype(o_ref.dtype)


def _merge_call(attn, yc, gates, w_attn, w_conv):
    m, d = attn.shape
    tm, tn = 1024, 1024
    nj = d // tn
    return pl.pallas_call(
        _merge_kernel,
        out_shape=jax.ShapeDtypeStruct((m, d), _BF16),
        grid=(m // tm, nj),
        in_specs=[pl.BlockSpec((tm, d), lambda i, j: (i, 0)),
                  pl.BlockSpec((tm, d), lambda i, j: (i, 0)),
                  pl.BlockSpec((tm, tn), lambda i, j: (i, j)),
                  pl.BlockSpec((tm, tn), lambda i, j: (i, nj + j)),
                  pl.BlockSpec((d, tn), lambda i, j: (0, j)),
                  pl.BlockSpec((d, tn), lambda i, j: (0, j))],
        out_specs=pl.BlockSpec((tm, tn), lambda i, j: (i, j)),
        compiler_params=_params(("parallel", "arbitrary")),
        name="merge",
    )(attn, yc, gates, gates, w_attn, w_conv)


def _outproj_kernel(m_ref, x_ref, mod_ref, w_ref, g_ref, b_ref, cast_ref, x1_ref, h2_ref, cast_o_ref):
    cast_o_ref[...] = cast_ref[...].astype(cast_o_ref.dtype)
    mix = jnp.dot(m_ref[...], w_ref[...], preferred_element_type=_F32)
    gate1 = mod_ref[0, 2:3, :]
    shift2 = mod_ref[0, 3:4, :]
    scale2 = mod_ref[0, 4:5, :]
    r = ALPHA * _ln(x_ref[...]) + gate1 * mix
    x1 = _ln(r) * g_ref[...] + b_ref[...]
    x1_ref[...] = x1
    h2_ref[...] = (x1 * (1.0 + scale2) + shift2).astype(h2_ref.dtype)


def _outproj_call(mixed, x2d, mod, w_out, ln_g, ln_b, w_cast, seq):
    m, d = mixed.shape
    tm = 512
    tiles_per_batch = seq // tm
    rows, cols = w_cast.shape
    rb = rows // (m // tm)
    return pl.pallas_call(
        _outproj_kernel,
        out_shape=(jax.ShapeDtypeStruct((m, d), _F32), jax.ShapeDtypeStruct((m, d), _BF16),
                   jax.ShapeDtypeStruct((rows, cols), _BF16)),
        grid=(m // tm,),
        in_specs=[pl.BlockSpec((tm, d), lambda i: (i, 0)),
                  pl.BlockSpec((tm, d), lambda i: (i, 0)),
                  pl.BlockSpec((1, 6, d), lambda i: (i // tiles_per_batch, 0, 0)),
                  pl.BlockSpec((d, d), lambda i: (0, 0)),
                  pl.BlockSpec((1, d), lambda i: (0, 0)),
                  pl.BlockSpec((1, d), lambda i: (0, 0)),
                  pl.BlockSpec((rb, cols), lambda i: (i, 0))],
        out_specs=(pl.BlockSpec((tm, d), lambda i: (i, 0)),
                   pl.BlockSpec((tm, d), lambda i: (i, 0)),
                   pl.BlockSpec((rb, cols), lambda i: (i, 0))),
        compiler_params=_params(("parallel",)),
        name="outproj",
    )(mixed, x2d, mod, w_out, ln_g, ln_b, w_cast)


def _ffn_kernel(h_ref, x1_hbm_ref, mod_ref, wg_ref, wu_ref, wo_ref, g_ref, b_ref, o_ref, a_ref, x1_ref, x1_sem):
    i = pl.program_id(0)
    k = pl.program_id(1)
    tm = o_ref.shape[0]
    x1_copy = pltpu.make_async_copy(x1_hbm_ref.at[pl.ds(pl.multiple_of(i * tm, tm), tm)], x1_ref, x1_sem)

    @pl.when(k == 0)
    def _():
        x1_copy.start()
        o_ref[...] = jnp.zeros_like(o_ref)

    h = h_ref[...]
    nc = 256
    for n in range(wg_ref.shape[1] // nc):
        sl = slice(n * nc, (n + 1) * nc)
        g = jnp.dot(h, wg_ref[:, sl], preferred_element_type=_F32)
        u = jnp.dot(h, wu_ref[:, sl], preferred_element_type=_F32)
        a_ref[:, sl] = (g * _sigmoid(g) * u).astype(a_ref.dtype)
    nc = 512
    for n in range(o_ref.shape[1] // nc):
        sl = slice(n * nc, (n + 1) * nc)
        o_ref[:, sl] += jnp.dot(a_ref[...], wo_ref[:, sl], preferred_element_type=_F32)

    @pl.when(k == pl.num_programs(1) - 1)
    def _():
        x1_copy.wait()
        gate2 = mod_ref[0, 5:6, :]
        r = ALPHA * x1_ref[...] + gate2 * o_ref[...]
        o_ref[...] = _ln(r) * g_ref[...] + b_ref[...]


def _ffn_call(h2, x1, mod, w_ffn_in, w_ffn_out, ln_g, ln_b, seq):
    m, d = h2.shape
    hidden = w_ffn_out.shape[0]
    tm, tk = 1024, 512
    nk = hidden // tk
    tiles_per_batch = seq // tm
    return pl.pallas_call(
        _ffn_kernel,
        out_shape=jax.ShapeDtypeStruct((m, d), _F32),
        grid=(m // tm, nk),
        in_specs=[pl.BlockSpec((tm, d), lambda i, k: (i, 0)),
                  pl.BlockSpec(memory_space=pl.ANY),
                  pl.BlockSpec((1, 6, d), lambda i, k: (i // tiles_per_batch, 0, 0)),
                  pl.BlockSpec((d, tk), lambda i, k: (0, k)),
                  pl.BlockSpec((d, tk), lambda i, k: (0, nk + k)),
                  pl.BlockSpec((tk, d), lambda i, k: (k, 0)),
                  pl.BlockSpec((1, d), lambda i, k: (0, 0)),
                  pl.BlockSpec((1, d), lambda i, k: (0, 0))],
        out_specs=pl.BlockSpec((tm, d), lambda i, k: (i, 0)),
        scratch_shapes=[pltpu.VMEM((tm, tk), _BF16),
                        pltpu.VMEM((tm, d), _F32),
                        pltpu.SemaphoreType.DMA(())],
        compiler_params=_params(("arbitrary", "arbitrary"), VMEM_LIMIT_MAX),
        name="ffn",
    )(h2, x1, mod, w_ffn_in, w_ffn_in, w_ffn_out, ln_g, ln_b)


def _rope_tables(seq):
    t = jnp.arange(seq, dtype=jnp.int32)
    n_freq = HEAD_DIM // 4
    inv_freq = ROPE_BASE ** (-jnp.arange(n_freq, dtype=_F32) / n_freq)
    ang_r = (t // GRID_W).astype(_F32)[:, None] * inv_freq
    ang_c = (t % GRID_W).astype(_F32)[:, None] * inv_freq
    zero = jnp.zeros_like(ang_r)
    cos = jnp.concatenate([jnp.cos(ang_r), jnp.cos(ang_r), jnp.cos(ang_c), jnp.cos(ang_c)], axis=1)
    sa = jnp.concatenate([-jnp.sin(ang_r), zero, -jnp.sin(ang_c), zero], axis=1)
    sb = jnp.concatenate([zero, jnp.sin(ang_r), zero, jnp.sin(ang_c)], axis=1)
    base = jnp.stack([cos, sa, sb])
    ident = jnp.stack([jnp.ones_like(cos), jnp.zeros_like(cos), jnp.zeros_like(cos)])
    return jnp.stack([base * (SCALE * LOG2E), base, ident])


def kernel(x, c, ctx, c_ctx, w_mod, b_mod, w_in, attn_sink, conv_w, conv_b, conv_norm_g, conv_norm_b,
           w_attn_proj, w_conv_proj, w_out, ln1_g, ln1_b, w_ffn_in, w_ffn_out, ln2_g, ln2_b):
    b, s, d = x.shape
    l = ctx.shape[1]
    m = b * s

    c_all = jnp.concatenate([c, c_ctx[None, :], jnp.zeros((16 - b - 1, d), _F32)], axis=0)
    mod_all = _mod_call(c_all, w_mod[0], b_mod[0][None, :])
    mod = mod_all[:b].reshape(b, 6, d)
    mod_c = mod_all[b:b + 1].reshape(1, 6, d)

    w_in0 = w_in[0]
    x2d = x.reshape(m, d)
    h = _ln_mod_call(x2d, mod, s, 0, 1, "ln_mod_x")
    hc = _ln_mod_call(ctx.reshape(b * l, d), mod_c, b * l, 0, 1, "ln_mod_ctx")

    tab = _rope_tables(s)
    qkv, w_rest = _qkv_call(h, w_in0, tab, s)
    qkv = qkv.reshape(b, s, OFF_GLU)
    kvc = _ctx_kv_call(hc, w_in0).reshape(b, l, 2 * KV_COLS)
    gates, w_ffn_in_b = _gate_call(h, w_rest, w_ffn_in[0])

    attn = _attn_call(qkv, kvc, attn_sink[0]).reshape(m, d)
    y, w_attn_b, w_conv_b, w_out_b = _glu_conv_call(
        h, w_rest, conv_w[0], conv_b[0][None, :], (w_attn_proj[0], w_conv_proj[0], w_out[0]), b, s)
    y = y.reshape(m, d)

    yc = _ln_silu_call(y, conv_norm_g[0][None, :], conv_norm_b[0][None, :])
    mixed = _merge_call(attn, yc, gates, w_attn_b, w_conv_b)
    x1, h2, w_ffn_out_b = _outproj_call(mixed, x2d, mod, w_out_b, ln1_g[0][None, :], ln1_b[0][None, :],
                                        w_ffn_out[0], s)
    out = _ffn_call(h2, x1, mod, w_ffn_in_b, w_ffn_out_b, ln2_g[0][None, :], ln2_b[0][None, :], s)
    return out.reshape(b, s, d)
```
